```python
import math
import jax
import jax.numpy as jnp
from jax import lax
import numpy as np

D_MODEL = 1024
BATCH = 16
SEQ = 2048
DEPTH = 2
DEC_BATCH = 128
DEC_SEQ = 8
PAST_LEN = 16384
PAGE_SIZE = 128

N_EVEN = (DEPTH + 1) // 2
N_ODD = DEPTH // 2
N_MEM = 256
EPS = 1e-6
ROPE_THETA = 10000.0
Q_BLOCK = 128

A_HEADS = 8
A_KV_HEADS = 2
A_HEAD_DIM = 64
IDX_HEADS = 8
IDX_DIM = 64
TOPK_MAX = 256
IDX_W_SCALE = (IDX_HEADS * IDX_DIM) ** -0.5

B_HEADS = 8
B_HEAD_DIM = 64
CONV_WIDTH = 4
GDN_CHUNK = 64
GDN_CONV_DIM = 3 * B_HEADS * B_HEAD_DIM

C_HEADS = 16
C_NOPE = 64
C_ROPE = 32
C_V = 64
C_Q_LORA = 384
C_KV_LORA = 256
MLA_SCALE = (C_NOPE + C_ROPE) ** -0.5

X_HEADS = 4
X_HEAD_DIM = 128
X_WIDTH = X_HEADS * X_HEAD_DIM
X_SCALE = X_HEAD_DIM ** -0.5

D_FF = 4 * D_MODEL

EVEN_SIZES = (A_HEADS * A_HEAD_DIM, A_KV_HEADS * A_HEAD_DIM, A_KV_HEADS * A_HEAD_DIM,
              IDX_HEADS * IDX_DIM, IDX_DIM, IDX_HEADS,
              B_HEADS * B_HEAD_DIM, B_HEADS * B_HEAD_DIM, B_HEADS * B_HEAD_DIM, B_HEADS * B_HEAD_DIM,
              B_HEADS, B_HEADS)
EVEN_IN = sum(EVEN_SIZES)
EVEN_MIX = A_HEADS * A_HEAD_DIM + B_HEADS * B_HEAD_DIM
ODD_IN = C_Q_LORA + C_KV_LORA + C_ROPE

kernel_name = 'hybrid_dsa_gdn_mla_step'


def rmsnorm(x, g):
    xf = x.astype(jnp.float32)
    y = xf * lax.rsqrt(jnp.mean(xf * xf, axis=-1, keepdims=True) + EPS)
    return (y * g.astype(jnp.float32)).astype(x.dtype)


def l2norm(x):
    xf = x.astype(jnp.float32)
    return xf * lax.rsqrt(jnp.sum(xf * xf, axis=-1, keepdims=True) + EPS)


def rope(x, pos):
    half = x.shape[-1] // 2
    inv = jnp.exp(-math.log(ROPE_THETA) * jnp.arange(half, dtype=jnp.float32) / half)
    ang = pos.astype(jnp.float32)[:, None] * inv[None, :]
    shp = (1, pos.shape[0]) + (1,) * (x.ndim - 3) + (half,)
    cos, sin = jnp.cos(ang).reshape(shp), jnp.sin(ang).reshape(shp)
    xf = x.astype(jnp.float32)
    x1, x2 = xf[..., :half], xf[..., half:]
    return jnp.concatenate([x1 * cos - x2 * sin, x2 * cos + x1 * sin], axis=-1).astype(x.dtype)


def split_cols(p, sizes):
    out, i = [], 0
    for s in sizes:
        out.append(p[..., i:i + s])
        i += s
    return out


def gather_rows(rows, idx):
    return jax.vmap(lambda r, i: r[i])(rows, idx)


def index_scores(qi, ki, wi):
    s = jax.nn.relu(jnp.einsum('bqhd,bsd->bqhs', qi, ki).astype(jnp.float32))
    return jnp.einsum('bqhs,bqh->bqs', s, wi.astype(jnp.float32))


def sparse_attend(q, kg, vg, valid):
    B, Q, H, hd = q.shape
    G = kg.shape[3]
    qg = q.reshape(B, Q, G, H // G, hd)
    s = jnp.einsum('bqgrd,bqkgd->bqgrk', qg, kg).astype(jnp.float32) * (hd ** -0.5)
    s = jnp.where(valid[:, :, None, None, :], s, -jnp.inf)
    p = jax.nn.softmax(s, axis=-1).astype(vg.dtype)
    return jnp.einsum('bqgrk,bqkgd->bqgrd', p, vg).reshape(B, Q, H * hd)


def dsa_prompt(q, k, v, qi, ki, wi):
    B, S = q.shape[:2]
    topk = min(TOPK_MAX, S // 4)
    key_pos = jnp.arange(S)

    def block(i):
        t0 = i * Q_BLOCK
        sl = lambda a: lax.dynamic_slice_in_dim(a, t0, Q_BLOCK, axis=1)
        qpos = t0 + jnp.arange(Q_BLOCK)
        sc = index_scores(sl(qi), ki, sl(wi))
        sc = jnp.where(key_pos[None, None, :] <= qpos[None, :, None], sc, -jnp.inf)
        _, idx = lax.top_k(sc, topk)
        valid = idx <= qpos[None, :, None]
        return sparse_attend(sl(q), gather_rows(k, idx), gather_rows(v, idx), valid)

    out = lax.map(block, jnp.arange(S // Q_BLOCK))
    return jnp.swapaxes(out, 0, 1).reshape(B, S, -1)


def dsa_sample(q, k, v, qi, ki, wi, pool_k, pool_v, pool_idx, page_table, layer):
    Bd, T = q.shape[:2]
    page = pool_k.shape[2]
    n_pages = page_table.shape[1]
    past_len = n_pages * page
    topk = min(TOPK_MAX, (past_len + T) // 4)

    def page_scores(j):
        return index_scores(qi, pool_idx[page_table[:, j], layer], wi)

    sp = lax.map(page_scores, jnp.arange(n_pages))
    sp = jnp.transpose(sp, (1, 2, 0, 3)).reshape(Bd, T, past_len)
    tq = jnp.arange(T)
    sn = jnp.where(tq[None, None, :] <= tq[None, :, None], index_scores(qi, ki, wi), -jnp.inf)
    _, idx = lax.top_k(jnp.concatenate([sp, sn], axis=-1), topk)
    valid = idx <= past_len + tq[None, :, None]
    is_past = (idx < past_len)[..., None, None]
    pidx = jnp.minimum(idx, past_len - 1)
    phys = jax.vmap(lambda pt, i: pt[i])(page_table, pidx // page)
    off = pidx % page
    nidx = jnp.clip(idx - past_len, 0, T - 1)
    kg = jnp.where(is_past, pool_k[phys, layer, off], gather_rows(k, nidx))
    vg = jnp.where(is_past, pool_v[phys, layer, off], gather_rows(v, nidx))
    return sparse_attend(q, kg, vg, valid)


def causal_conv(xin, buf, w):
    L = xin.shape[1]
    xp = jnp.concatenate([buf.astype(xin.dtype), xin], axis=1)
    y = xp[:, :L] * w[0]
    for i in range(1, CONV_WIDTH):
        y = y + xp[:, i:i + L] * w[i]
    return jax.nn.silu(y), xp[:, L:]


def gated_delta(q, k, v, g, beta, s0):
    B, L, H, dk = q.shape
    dv = v.shape[-1]
    C = math.gcd(L, GDN_CHUNK)
    n = L // C

    def chunk(a):
        return jnp.moveaxis(a.reshape((B, n, C) + a.shape[2:]), 3, 2)

    qc, kc, vc, bc = chunk(q), chunk(k), chunk(v), chunk(beta)
    gc = jnp.cumsum(chunk(g), axis=-1)
    ti = jnp.arange(C)
    incl = ti[None, :] <= ti[:, None]
    strict = ti[None, :] < ti[:, None]
    decay = jnp.exp(jnp.where(incl, gc[..., :, None] - gc[..., None, :], -jnp.inf))
    kk = jnp.einsum('bnhtd,bnhid->bnhti', kc, kc)
    lmat = jnp.where(strict, bc[..., :, None] * kk * decay, 0.0) + jnp.eye(C, dtype=jnp.float32)
    gam = jnp.exp(gc)
    rhs = jnp.concatenate([vc * bc[..., None], kc * (bc * gam)[..., None]], axis=-1)
    sol = lax.linalg.triangular_solve(lmat, rhs, left_side=True, lower=True, unit_diagonal=True)
    u, w = sol[..., :dv], sol[..., dv:]
    aqk = jnp.einsum('bnhtd,bnhid->bnhti', qc, kc) * decay
    qg = qc * gam[..., None]
    kd = kc * jnp.exp(gc[..., -1:] - gc)[..., None]
    glast = jnp.exp(gc[..., -1])
    xs = tuple(jnp.moveaxis(a, 1, 0) for a in (u, w, aqk, qg, kd, glast))

    def step(S, inp):
        u_, w_, aqk_, qg_, kd_, gl_ = inp
        delta = u_ - jnp.einsum('bhcd,bhde->bhce', w_, S)
        o = jnp.einsum('bhcd,bhde->bhce', qg_, S) + jnp.einsum('bhti,bhie->bhte', aqk_, delta)
        S = S * gl_[..., None, None] + jnp.einsum('bhcd,bhce->bhde', kd_, delta)
        return S, o

    s_fin, o = lax.scan(step, s0, xs)
    return jnp.transpose(o, (1, 0, 3, 2, 4)).reshape(B, L, H, dv), s_fin


def gdn_mixer(bq, bk, bv, bz, bb, ba, conv_buf, s0, conv_w, a_log, dt_bias, norm_g):
    B, L = bq.shape[:2]
    nqk = B_HEADS * B_HEAD_DIM
    xc, new_buf = causal_conv(jnp.concatenate([bq, bk, bv], axis=-1), conv_buf, conv_w)
    shp = (B, L, B_HEADS, B_HEAD_DIM)
    q = l2norm(xc[..., :nqk].reshape(shp)) * (B_HEAD_DIM ** -0.5)
    k = l2norm(xc[..., nqk:2 * nqk].reshape(shp))
    v = xc[..., 2 * nqk:].reshape(shp).astype(jnp.float32)
    g = -jnp.exp(a_log.astype(jnp.float32)) * jax.nn.softplus(ba.astype(jnp.float32) + dt_bias.astype(jnp.float32))
    beta = jax.nn.sigmoid(bb.astype(jnp.float32))
    o, s_new = gated_delta(q, k, v, g, beta, s0.astype(jnp.float32))
    o = rmsnorm(o, norm_g) * jax.nn.silu(bz.astype(jnp.float32).reshape(shp))
    return o.reshape(B, L, -1).astype(bq.dtype), new_buf, s_new.astype(s0.dtype)


def even_project(h, pos, w_in):
    B, L = h.shape[:2]
    aq, ak, av, iq, ik, iw, bq, bk, bv, bz, bb, ba = split_cols(h @ w_in, EVEN_SIZES)
    aq = rope(aq.reshape(B, L, A_HEADS, A_HEAD_DIM), pos)
    ak = rope(ak.reshape(B, L, A_KV_HEADS, A_HEAD_DIM), pos)
    av = av.reshape(B, L, A_KV_HEADS, A_HEAD_DIM)
    iq = rope(iq.reshape(B, L, IDX_HEADS, IDX_DIM), pos)
    ik = rope(ik, pos)
    return (aq, ak, av, iq, ik, iw * IDX_W_SCALE), (bq, bk, bv, bz, bb, ba)


def mla_project(h, pos, w_down, g_q, g_kv, w_uq):
    cq, ckv, kr = split_cols(h @ w_down, (C_Q_LORA, C_KV_LORA, C_ROPE))
    q = jnp.einsum('blr,rhd->blhd', rmsnorm(cq, g_q), w_uq)
    return q[..., :C_NOPE], rope(q[..., C_NOPE:], pos), rmsnorm(ckv, g_kv), rope(kr, pos)


def mla_prompt(qn, qr, ckv, kr, w_ukv):
    B, S = qn.shape[:2]
    kv = jnp.einsum('bsr,rhd->bshd', ckv, w_ukv)
    kn, v = kv[..., :C_NOPE], kv[..., C_NOPE:]
    key_pos = jnp.arange(S)

    def block(i):
        t0 = i * Q_BLOCK
        qpos = t0 + jnp.arange(Q_BLOCK)
        qnb = lax.dynamic_slice_in_dim(qn, t0, Q_BLOCK, axis=1)
        qrb = lax.dynamic_slice_in_dim(qr, t0, Q_BLOCK, axis=1)
        s = (jnp.einsum('bqhd,bshd->bhqs', qnb, kn) + jnp.einsum('bqhd,bsd->bhqs', qrb, kr)).astype(jnp.float32) * MLA_SCALE
        s = jnp.where(key_pos[None, None, None, :] <= qpos[None, None, :, None], s, -jnp.inf)
        p = jax.nn.softmax(s, axis=-1).astype(v.dtype)
        return jnp.einsum('bhqs,bshd->bqhd', p, v).reshape(B, Q_BLOCK, -1)

    out = lax.map(block, jnp.arange(S // Q_BLOCK))
    return jnp.swapaxes(out, 0, 1).reshape(B, S, -1)


def mla_sample(qn, qr, ckv, kr, w_ukv, pool_lat, pool_kr, page_table, layer):
    Bd, T = qn.shape[:2]
    q_lat = jnp.einsum('bthd,rhd->bthr', qn, w_ukv[..., :C_NOPE])

    def scores(lat, kro):
        return (jnp.einsum('bthr,bsr->bths', q_lat, lat) + jnp.einsum('bthd,bsd->bths', qr, kro)).astype(jnp.float32) * MLA_SCALE

    tq = jnp.arange(T)
    s = jnp.where(tq[None, None, None, :] <= tq[None, :, None, None], scores(ckv, kr), -jnp.inf)
    m = jnp.max(s, axis=-1)
    p = jnp.exp(s - m[..., None])
    l = jnp.sum(p, axis=-1)
    acc = jnp.einsum('bths,bsr->bthr', p, ckv.astype(jnp.float32))

    def step(carry, j):
        m, l, acc = carry
        phys = page_table[:, j]
        lat = pool_lat[phys, layer]
        s = scores(lat, pool_kr[phys, layer])
        m_new = jnp.maximum(m, jnp.max(s, axis=-1))
        corr = jnp.exp(m - m_new)
        p = jnp.exp(s - m_new[..., None])
        acc = acc * corr[..., None] + jnp.einsum('bths,bsr->bthr', p, lat.astype(jnp.float32))
        return (m_new, l * corr + jnp.sum(p, axis=-1), acc), None

    (m, l, acc), _ = lax.scan(step, (m, l, acc), jnp.arange(page_table.shape[1]))
    o = (acc / l[..., None]).astype(qn.dtype)
    return jnp.einsum('bthr,rhd->bthd', o, w_ukv[..., C_NOPE:]).reshape(Bd, T, -1)


def memory_kv(mem, g, w_k, w_v):
    B, M, _ = mem.shape
    m = rmsnorm(mem, g)
    return (m @ w_k).reshape(B, M, X_HEADS, X_HEAD_DIM), (m @ w_v).reshape(B, M, X_HEADS, X_HEAD_DIM)


def cross_attend(h, mk, mv, w_q, w_o):
    B, L, _ = h.shape
    q = (h @ w_q).reshape(B, L, X_HEADS, X_HEAD_DIM)
    s = jnp.einsum('blhd,bmhd->blhm', q, mk).astype(jnp.float32) * X_SCALE
    p = jax.nn.softmax(s, axis=-1).astype(mv.dtype)
    return jnp.einsum('blhm,bmhd->blhd', p, mv).reshape(B, L, -1) @ w_o


def sq_relu_mlp(h, w1, w2):
    return jnp.square(jax.nn.relu(h @ w1)) @ w2


def setup_inputs(seed: int = 0) -> dict:
    key = jax.random.key(seed)
    k = jax.random.split(key, 40)
    f32 = jnp.float32
    n_pages = PAST_LEN // PAGE_SIZE
    n_used = DEC_BATCH * n_pages
    n_pool = n_used + max(1, n_used // 4)

    def nrm(i, shape, scale=1.0):
        return jax.random.normal(k[i], shape, f32) * scale

    def gain(i, shape):
        return 1.0 + 0.02 * jax.random.normal(k[i], shape, f32)

    page_table = jax.random.permutation(k[0], n_pool)[:n_used].reshape(DEC_BATCH, n_pages).astype(jnp.int32)
    dt = jnp.exp(jax.random.uniform(k[1], (N_EVEN, B_HEADS), f32, math.log(1e-3), math.log(1e-1)))
    gdn_dt_bias = dt + jnp.log(-jnp.expm1(-dt))
    gdn_a_log = jnp.log(jax.random.uniform(k[2], (N_EVEN, B_HEADS), f32, 1.0, 16.0))
    return {
        'x_prompt': nrm(3, (BATCH, SEQ, D_MODEL)),
        'x_sample': nrm(4, (DEC_BATCH, DEC_SEQ, D_MODEL)),
        'cache_a_k': nrm(5, (n_pool, N_EVEN, PAGE_SIZE, A_KV_HEADS, A_HEAD_DIM)),
        'cache_a_v': nrm(6, (n_pool, N_EVEN, PAGE_SIZE, A_KV_HEADS, A_HEAD_DIM)),
        'cache_a_idx': nrm(7, (n_pool, N_EVEN, PAGE_SIZE, IDX_DIM)),
        'state_b_ssm': nrm(8, (N_EVEN, DEC_BATCH, B_HEADS, B_HEAD_DIM, B_HEAD_DIM), 0.1),
        'state_b_conv': nrm(9, (N_EVEN, DEC_BATCH, CONV_WIDTH - 1, GDN_CONV_DIM)),
        'cache_c_latent': nrm(10, (n_pool, N_ODD, PAGE_SIZE, C_KV_LORA)),
        'cache_c_krope': nrm(11, (n_pool, N_ODD, PAGE_SIZE, C_ROPE)),
        'cache_mem_k': nrm(12, (DEPTH, DEC_BATCH, N_MEM, X_HEADS, X_HEAD_DIM)),
        'cache_mem_v': nrm(13, (DEPTH, DEC_BATCH, N_MEM, X_HEADS, X_HEAD_DIM)),
        'page_table': page_table,
        'mem_prompt': nrm(14, (BATCH, N_MEM, D_MODEL)),
        'g_mix': gain(15, (DEPTH, D_MODEL)),
        'g_cross': gain(16, (DEPTH, D_MODEL)),
        'g_mem': gain(17, (DEPTH, D_MODEL)),
        'g_mlp': gain(18, (DEPTH, D_MODEL)),
        'g_final': gain(19, (D_MODEL,)),
        'w_in_even': nrm(20, (N_EVEN, D_MODEL, EVEN_IN), D_MODEL ** -0.5),
        'gdn_conv_w': nrm(21, (N_EVEN, CONV_WIDTH, GDN_CONV_DIM), CONV_WIDTH ** -0.5),
        'gdn_a_log': gdn_a_log,
        'gdn_dt_bias': gdn_dt_bias,
        'gdn_norm_g': gain(22, (N_EVEN, B_HEAD_DIM)),
        'w_out_even': nrm(23, (N_EVEN, EVEN_MIX, D_MODEL), EVEN_MIX ** -0.5),
        'w_down_odd': nrm(24, (N_ODD, D_MODEL, ODD_IN), D_MODEL ** -0.5),
        'g_q_lora': gain(25, (N_ODD, C_Q_LORA)),
        'g_kv_lora': gain(26, (N_ODD, C_KV_LORA)),
        'w_uq': nrm(27, (N_ODD, C_Q_LORA, C_HEADS, C_NOPE + C_ROPE), C_Q_LORA ** -0.5),
        'w_ukv': nrm(28, (N_ODD, C_KV_LORA, C_HEADS, C_NOPE + C_V), C_KV_LORA ** -0.5),
        'w_out_odd': nrm(29, (N_ODD, C_HEADS * C_V, D_MODEL), (C_HEADS * C_V) ** -0.5),
        'w_xq': nrm(30, (DEPTH, D_MODEL, X_WIDTH), D_MODEL ** -0.5),
        'w_xk': nrm(31, (DEPTH, D_MODEL, X_WIDTH), D_MODEL ** -0.5),
        'w_xv': nrm(32, (DEPTH, D_MODEL, X_WIDTH), D_MODEL ** -0.5),
        'w_xo': nrm(33, (DEPTH, X_WIDTH, D_MODEL), X_WIDTH ** -0.5),
        'w_ff1': nrm(34, (DEPTH, D_MODEL, D_FF), D_MODEL ** -0.5),
        'w_ff2': nrm(35, (DEPTH, D_FF, D_MODEL), D_FF ** -0.5),
    }


def reference(x_prompt, x_sample, cache_a_k, cache_a_v, cache_a_idx, state_b_ssm, state_b_conv,
              cache_c_latent, cache_c_krope, cache_mem_k, cache_mem_v, page_table, mem_prompt,
              g_mix, g_cross, g_mem, g_mlp, g_final,
              w_in_even, gdn_conv_w, gdn_a_log, gdn_dt_bias, gdn_norm_g, w_out_even,
              w_down_odd, g_q_lora, g_kv_lora, w_uq, w_ukv, w_out_odd,
              w_xq, w_xk, w_xv, w_xo, w_ff1, w_ff2):
    Bp, S, _ = x_prompt.shape
    Bd, T, _ = x_sample.shape
    past_len = page_table.shape[1] * cache_a_k.shape[2]
    pos_p = jnp.arange(S, dtype=jnp.int32)
    pos_s = past_len + jnp.arange(T, dtype=jnp.int32)
    xp, xs = x_prompt, x_sample
    pa_k, pa_v, pa_i, pb_s, pb_c, pc_l, pc_r, pm_k, pm_v = [], [], [], [], [], [], [], [], []
    sa_k, sa_v, sa_i, sb_s, sb_c, sc_l, sc_r = [], [], [], [], [], [], []
    for li in range(DEPTH):
        hp, hs = rmsnorm(xp, g_mix[li]), rmsnorm(xs, g_mix[li])
        if li % 2 == 0:
            e = li // 2
            (aq, ak, av, iq, ik, iw), bpr = even_project(hp, pos_p, w_in_even[e])
            a_out = dsa_prompt(aq, ak, av, iq, ik, iw)
            b_out, conv_new, ssm_new = gdn_mixer(
                *bpr, jnp.zeros((Bp, CONV_WIDTH - 1, GDN_CONV_DIM), hp.dtype),
                jnp.zeros((Bp, B_HEADS, B_HEAD_DIM, B_HEAD_DIM), jnp.float32),
                gdn_conv_w[e], gdn_a_log[e], gdn_dt_bias[e], gdn_norm_g[e])
            xp = xp + jnp.concatenate([a_out, b_out], axis=-1) @ w_out_even[e]
            pa_k.append(ak); pa_v.append(av); pa_i.append(ik); pb_s.append(ssm_new); pb_c.append(conv_new)
            (aq, ak, av, iq, ik, iw), bsm = even_project(hs, pos_s, w_in_even[e])
            a_out = dsa_sample(aq, ak, av, iq, ik, iw, cache_a_k, cache_a_v, cache_a_idx, page_table, e)
            b_out, conv_new, ssm_new = gdn_mixer(
                *bsm, state_b_conv[e], state_b_ssm[e],
                gdn_conv_w[e], gdn_a_log[e], gdn_dt_bias[e], gdn_norm_g[e])
            xs = xs + jnp.concatenate([a_out, b_out], axis=-1) @ w_out_even[e]
            sa_k.append(ak); sa_v.append(av); sa_i.append(ik); sb_s.append(ssm_new); sb_c.append(conv_new)
        else:
            o = li // 2
            qn, qr, ckv, kr = mla_project(hp, pos_p, w_down_odd[o], g_q_lora[o], g_kv_lora[o], w_uq[o])
            xp = xp + mla_prompt(qn, qr, ckv, kr, w_ukv[o]) @ w_out_odd[o]
            pc_l.append(ckv); pc_r.append(kr)
            qn, qr, ckv, kr = mla_project(hs, pos_s, w_down_odd[o], g_q_lora[o], g_kv_lora[o], w_uq[o])
            xs = xs + mla_sample(qn, qr, ckv, kr, w_ukv[o], cache_c_latent, cache_c_krope, page_table, o) @ w_out_odd[o]
            sc_l.append(ckv); sc_r.append(kr)
        mk, mv = memory_kv(mem_prompt, g_mem[li], w_xk[li], w_xv[li])
        pm_k.append(mk); pm_v.append(mv)
        xp = xp + cross_attend(rmsnorm(xp, g_cross[li]), mk, mv, w_xq[li], w_xo[li])
        xs = xs + cross_attend(rmsnorm(xs, g_cross[li]), cache_mem_k[li], cache_mem_v[li], w_xq[li], w_xo[li])
        xp = xp + sq_relu_mlp(rmsnorm(xp, g_mlp[li]), w_ff1[li], w_ff2[li])
        xs = xs + sq_relu_mlp(rmsnorm(xs, g_mlp[li]), w_ff1[li], w_ff2[li])
    y_prompt = rmsnorm(xp, g_final)
    y_sample = rmsnorm(xs, g_final)
    p_a_k = jnp.stack(pa_k, axis=1)
    p_a_v = jnp.stack(pa_v, axis=1)
    p_a_idx = jnp.stack(pa_i, axis=1)
    p_b_ssm = jnp.stack(pb_s, axis=0)
    p_b_conv = jnp.stack(pb_c, axis=0)
    p_c_latent = jnp.stack(pc_l, axis=1)
    p_c_krope = jnp.stack(pc_r, axis=1)
    p_mem_k = jnp.stack(pm_k, axis=0)
    p_mem_v = jnp.stack(pm_v, axis=0)
    s_a_k = jnp.stack(sa_k, axis=1)
    s_a_v = jnp.stack(sa_v, axis=1)
    s_a_idx = jnp.stack(sa_i, axis=1)
    s_b_ssm = jnp.stack(sb_s, axis=0)
    s_b_conv = jnp.stack(sb_c, axis=0)
    s_c_latent = jnp.stack(sc_l, axis=1)
    s_c_krope = jnp.stack(sc_r, axis=1)
    return (y_prompt, y_sample, p_a_k, p_a_v, p_a_idx, p_b_ssm, p_b_conv, p_c_latent, p_c_krope,
            p_mem_k, p_mem_v, s_a_k, s_a_v, s_a_idx, s_b_ssm, s_b_conv, s_c_latent, s_c_krope)
```

```python
import functools
import math

import jax
import jax.numpy as jnp
from jax import lax
from jax.experimental import pallas as pl
from jax.experimental.pallas import tpu as pltpu

F32 = jnp.float32
BF16 = jnp.bfloat16
I32 = jnp.int32

LANES = 128
SUBLANES = 8
VMEM_LIMIT = 56 * 1024 * 1024

EPS = 1e-6
ROPE_THETA = 10000.0
N_MEM = 256
TOPK_MAX = 256

A_HEADS = 8
A_KV_HEADS = 2
A_HEAD_DIM = 64
IDX_HEADS = 8
IDX_DIM = 64
IDX_W_SCALE = (IDX_HEADS * IDX_DIM) ** -0.5

B_HEADS = 8
B_HEAD_DIM = 64
CONV_WIDTH = 4
GDN_CHUNK = 64
GDN_QK = B_HEADS * B_HEAD_DIM
GDN_CONV_DIM = 3 * GDN_QK

C_HEADS = 16
C_NOPE = 64
C_ROPE = 32
C_V = 64
C_Q_LORA = 384
C_KV_LORA = 256
MLA_SCALE = (C_NOPE + C_ROPE) ** -0.5

X_HEADS = 4
X_HEAD_DIM = 128
X_WIDTH = X_HEADS * X_HEAD_DIM
X_SCALE = X_HEAD_DIM ** -0.5

INT_MIN = -2 ** 31
NEG_BIG = -1e30


def _params(*sem):
    return pltpu.CompilerParams(dimension_semantics=sem, vmem_limit_bytes=VMEM_LIMIT)


def _rms(x, g):
    return x * lax.rsqrt(jnp.mean(x * x, axis=-1, keepdims=True) + EPS) * g


def _dot(a, b):
    return jnp.dot(a, b, preferred_element_type=F32)


def _dot_nt(a, b):
    return lax.dot_general(a, b, (((1,), (1,)), ((), ())), preferred_element_type=F32)


def _split3(a):
    a1 = a.astype(BF16)
    r = a - a1.astype(F32)
    a2 = r.astype(BF16)
    a3 = (r - a2.astype(F32)).astype(BF16)
    return a1, a2, a3


def _dot_hi(a, b, nt=False):
    f = _dot_nt if nt else _dot
    a1, a2, a3 = _split3(a)
    b1, b2, b3 = _split3(b)
    small = f(a1, b3) + f(a3, b1) + f(a2, b2)
    mid = f(a1, b2) + f(a2, b1)
    return f(a1, b1) + (mid + small)


def _dot_exact_rhs(a, b_bf16):
    a1, a2, a3 = _split3(a)
    return _dot(a1, b_bf16) + (_dot(a2, b_bf16) + _dot(a3, b_bf16))


def _proj_kernel(*refs, groups, has_gain, has_rope, has_scale, has_res, emit_norm, shift):
    it = iter(refs)
    x_ref = next(it)
    g_ref = next(it) if has_gain else None
    w_ref = next(it)
    if has_rope:
        c_ref, sa_ref, sb_ref = next(it), next(it), next(it)
    s_ref = next(it) if has_scale else None
    r_ref = next(it) if has_res else None
    outs = list(it)
    x = x_ref[...].astype(F32)
    if has_gain:
        x = _rms(x, g_ref[...])
    if emit_norm:
        outs[-1][...] = x
    xb = x.astype(BF16)
    col = 0
    for (width, rope, _), o_ref in zip(groups, outs):
        y = _dot(xb, w_ref[:, col:col + width])
        if has_scale:
            y = y * s_ref[:, col:col + width]
        if rope:
            n = width // LANES
            c = jnp.concatenate([c_ref[...]] * n, axis=1)
            sa = jnp.concatenate([sa_ref[...]] * n, axis=1)
            sb = jnp.concatenate([sb_ref[...]] * n, axis=1)
            y = y * c + pltpu.roll(y, shift, 1) * sa + pltpu.roll(y, width - shift, 1) * sb
        if has_res:
            y = y + r_ref[...]
        o_ref[...] = y.astype(o_ref.dtype)
        col += width


def _proj(x, w, groups, *, gain=None, rope=None, colscale=None, residual=None,
          emit_norm=False, tm=256, name="proj"):
    M, K = x.shape
    N = w.shape[1]
    assert sum(g[0] for g in groups) == N and M % tm == 0
    args, specs = [x], [pl.BlockSpec((tm, K), lambda i: (i, 0))]
    if gain is not None:
        args.append(gain.reshape(1, K).astype(F32))
        specs.append(pl.BlockSpec((1, K), lambda i: (0, 0)))
    args.append(w)
    specs.append(pl.BlockSpec((K, N), lambda i: (0, 0)))
    shift = 0
    if rope is not None:
        c, sa, sb, shift, nblk = rope
        for t in (c, sa, sb):
            args.append(t)
            specs.append(pl.BlockSpec((tm, LANES), lambda i, nblk=nblk: (i % nblk, 0)))
    if colscale is not None:
        args.append(colscale.reshape(1, N).astype(F32))
        specs.append(pl.BlockSpec((1, N), lambda i: (0, 0)))
    if residual is not None:
        assert len(groups) == 1
        args.append(residual)
        specs.append(pl.BlockSpec((tm, N), lambda i: (i, 0)))
    out_shape = [jax.ShapeDtypeStruct((M, g[0]), g[2]) for g in groups]
    out_specs = [pl.BlockSpec((tm, g[0]), lambda i: (i, 0)) for g in groups]
    if emit_norm:
        out_shape.append(jax.ShapeDtypeStruct((M, K), F32))
        out_specs.append(pl.BlockSpec((tm, K), lambda i: (i, 0)))
    kern = functools.partial(
        _proj_kernel, groups=tuple(groups), has_gain=gain is not None,
        has_rope=rope is not None, has_scale=colscale is not None,
        has_res=residual is not None, emit_norm=emit_norm, shift=shift)
    return pl.pallas_call(
        kern, grid=(M // tm,), in_specs=specs, out_specs=out_specs,
        out_shape=out_shape, compiler_params=_params("parallel"), name=name)(*args)


def _mlp_kernel(x_ref, g_ref, w1_ref, w2_ref, gf_ref, o_ref, hn_ref, acc_ref, *, final_norm):
    j = pl.program_id(1)

    @pl.when(j == 0)
    def _():
        hn_ref[...] = _rms(x_ref[...], g_ref[...]).astype(BF16)
        acc_ref[...] = jnp.zeros_like(acc_ref)

    a = _dot(hn_ref[...], w1_ref[...])
    a = jnp.square(jnp.maximum(a, 0.0)).astype(BF16)
    acc_ref[...] += _dot(a, w2_ref[...])

    @pl.when(j == pl.num_programs(1) - 1)
    def _():
        y = x_ref[...] + acc_ref[...]
        if final_norm:
            y = _rms(y, gf_ref[...])
        o_ref[...] = y


def _mlp(x, g, w1, w2, g_final, *, final_norm, tm=512, tf=1024):
    M, D = x.shape
    F = w1.shape[1]
    tm = min(tm, M)
    assert M % tm == 0 and F % tf == 0
    return pl.pallas_call(
        functools.partial(_mlp_kernel, final_norm=final_norm),
        grid=(M // tm, F // tf),
        in_specs=[pl.BlockSpec((tm, D), lambda i, j: (i, 0)),
                  pl.BlockSpec((1, D), lambda i, j: (0, 0)),
                  pl.BlockSpec((D, tf), lambda i, j: (0, j)),
                  pl.BlockSpec((tf, D), lambda i, j: (j, 0)),
                  pl.BlockSpec((1, D), lambda i, j: (0, 0))],
        out_specs=pl.BlockSpec((tm, D), lambda i, j: (i, 0)),
        out_shape=jax.ShapeDtypeStruct((M, D), F32),
        scratch_shapes=[pltpu.VMEM((tm, D), BF16), pltpu.VMEM((tm, D), F32)],
        compiler_params=_params("parallel", "arbitrary"), name="mlp",
    )(x, g.reshape(1, D), w1, w2, g_final.reshape(1, D))


def _xattn_kernel(x_ref, g_ref, wq_ref, mk_ref, mv_ref, wo_ref, o_ref):
    x = x_ref[...]
    q = _dot(_rms(x, g_ref[...]).astype(BF16), wq_ref[...])
    mk = mk_ref[0].astype(BF16)
    mv = mv_ref[0].astype(BF16)
    heads = []
    for h in range(X_HEADS):
        sl = slice(h * X_HEAD_DIM, (h + 1) * X_HEAD_DIM)
        s = _dot_nt(q[:, sl].astype(BF16), mk[:, sl]) * X_SCALE
        s = s - jnp.max(s, axis=-1, keepdims=True)
        p = jnp.exp(s)
        p = p / jnp.sum(p, axis=-1, keepdims=True)
        heads.append(_dot(p.astype(BF16), mv[:, sl]))
    o = jnp.concatenate(heads, axis=1).astype(BF16)
    o_ref[...] = x + _dot(o, wo_ref[...])


def _xattn(x, g, wq, mk, mv, wo, *, rows_per_batch, tm):
    M, D = x.shape
    assert rows_per_batch % tm == 0
    per = rows_per_batch // tm
    nm = mk.shape[1]
    return pl.pallas_call(
        _xattn_kernel, grid=(M // tm,),
        in_specs=[pl.BlockSpec((tm, D), lambda i: (i, 0)),
                  pl.BlockSpec((1, D), lambda i: (0, 0)),
                  pl.BlockSpec((D, X_WIDTH), lambda i: (0, 0)),
                  pl.BlockSpec((1, nm, X_WIDTH), lambda i: (i // per, 0, 0)),
                  pl.BlockSpec((1, nm, X_WIDTH), lambda i: (i // per, 0, 0)),
                  pl.BlockSpec((X_WIDTH, D), lambda i: (0, 0))],
        out_specs=pl.BlockSpec((tm, D), lambda i: (i, 0)),
        out_shape=jax.ShapeDtypeStruct((M, D), F32),
        compiler_params=_params("parallel"), name="xattn",
    )(x, g.reshape(1, D), wq, mk, mv, wo)


def _rope_tables(pos, dim, lane_lo, lane_hi):
    half = dim // 2
    lane = jnp.arange(LANES)
    j = (lane - lane_lo) % dim
    inside = (lane >= lane_lo) & (lane < lane_hi)
    inv = jnp.exp(-math.log(ROPE_THETA) * (j % half).astype(F32) / half)
    ang = pos.astype(F32)[:, None] * inv[None, :]
    cos, sin = jnp.cos(ang), jnp.sin(ang)
    c = jnp.where(inside[None, :], cos, 1.0)
    sa = jnp.where((inside & (j >= half))[None, :], sin, 0.0)
    sb = jnp.where((inside & (j < half))[None, :], -sin, 0.0)
    return c, sa, sb, half


def _mla_prompt_kernel(q_ref, kv_ref, kr_ref, o_ref, *, tq):
    i = pl.program_id(2)
    S = kv_ref.shape[0]
    lane = lax.broadcasted_iota(I32, (S, LANES), 1)
    kr = kr_ref[...]
    qpos = i * tq + lax.broadcasted_iota(I32, (tq, S), 0)
    kpos = lax.broadcasted_iota(I32, (tq, S), 1)
    causal = kpos <= qpos
    outs = []
    for h in range(2):
        kv = kv_ref[:, h * LANES:(h + 1) * LANES]
        kcat = jnp.where(lane < C_NOPE, kv, kr)
        s = _dot_nt(q_ref[:, h * LANES:(h + 1) * LANES], kcat) * MLA_SCALE
        s = jnp.where(causal, s, -jnp.inf)
        p = jnp.exp(s - jnp.max(s, axis=-1, keepdims=True))
        l = jnp.sum(p, axis=-1, keepdims=True)
        outs.append(_dot(p.astype(BF16), kv) / l)
    lane_o = lax.broadcasted_iota(I32, (tq, LANES), 1)
    o_ref[...] = jnp.where(lane_o < C_V, pltpu.roll(outs[0], C_V, 1), outs[1]).astype(o_ref.dtype)


def _mla_prompt(q, kv, kr, *, B, S, tq=256):
    tq = min(tq, S)
    nq = S // tq
    return pl.pallas_call(
        functools.partial(_mla_prompt_kernel, tq=tq),
        grid=(B, C_HEADS // 2, nq),
        in_specs=[pl.BlockSpec((tq, 2 * LANES), lambda b, h, i: (b * nq + i, h)),
                  pl.BlockSpec((S, 2 * LANES), lambda b, h, i: (b, h)),
                  pl.BlockSpec((S, LANES), lambda b, h, i: (b, 0))],
        out_specs=pl.BlockSpec((tq, LANES), lambda b, h, i: (b * nq + i, h)),
        out_shape=jax.ShapeDtypeStruct((B * S, C_HEADS * C_V), BF16),
        compiler_params=_params("parallel", "parallel", "arbitrary"), name="mla_prompt",
    )(q, kv, kr)


def _order_key(sc):
    sc = jnp.where(sc == 0.0, 0.0, sc)
    bits = lax.bitcast_convert_type(sc, I32)
    return bits ^ ((bits >> 31) & 0x7FFFFFFF)


def _topk_mask(key, idx, k, axis, n_idx_bits):
    def cnt(m):
        return jnp.sum(m.astype(F32), axis=axis, keepdims=True)

    kf = float(k)
    t0 = jnp.where(cnt(key >= 0) >= kf, 0, INT_MIN).astype(I32)

    def vbody(n, t):
        cand = t | jnp.left_shift(jnp.int32(1), 30 - n)
        return jnp.where(cnt(key >= cand) >= kf, cand, t)

    t = lax.fori_loop(0, 31, vbody, t0)
    need = kf - cnt(key > t)
    eq = key == t

    def ibody(n, x):
        cand = x | jnp.left_shift(jnp.int32(1), n_idx_bits - 1 - n)
        return jnp.where(cnt(eq & (idx < cand)) < need, cand, x)

    x = lax.fori_loop(0, n_idx_bits, ibody, jnp.zeros_like(t))
    return (key > t) | (eq & (idx <= x))


def _dsa_prompt_kernel(iq_ref, ik_ref, iw_ref, aq_ref, ak_ref, avt_ref, o_ref, *, tq, topk):
    i = pl.program_id(1)
    S = ik_ref.shape[0]
    kidx = lax.broadcasted_iota(I32, (S, tq), 0)
    causal = kidx <= i * tq + lax.broadcasted_iota(I32, (S, tq), 1)
    ik = ik_ref[...].astype(BF16)
    sc = jnp.zeros((S, tq), F32)
    for h in range(IDX_HEADS):
        s = _dot_nt(ik, iq_ref[:, h * LANES:(h + 1) * LANES])
        sc = sc + iw_ref[h:h + 1, :] * jnp.maximum(s, 0.0)
    key = jnp.where(causal, _order_key(sc), INT_MIN)
    sel = _topk_mask(key, kidx, topk, 0, max(1, (S - 1).bit_length())) & causal

    ak = ak_ref[...].astype(BF16)
    avt = avt_ref[0].astype(BF16)
    rep = A_HEADS // A_KV_HEADS
    outs = []
    for h in range(A_HEADS):
        s = _dot_nt(ak, aq_ref[:, h * LANES:(h + 1) * LANES]) * (A_HEAD_DIM ** -0.5)
        s = jnp.where(sel, s, -jnp.inf)
        p = jnp.exp(s - jnp.max(s, axis=0, keepdims=True))
        l = jnp.sum(p, axis=0, keepdims=True)
        ot = _dot(avt, p.astype(BF16))
        g = h // rep
        outs.append(ot[g * A_HEAD_DIM:(g + 1) * A_HEAD_DIM, :] / l)
    o_ref[...] = jnp.concatenate(outs, axis=0).T.astype(o_ref.dtype)


def _dsa_prompt(iq, ik, iw_t, aq, ak, av_t, *, B, S, tq=256):
    tq = min(tq, S)
    nq = S // tq
    topk = min(TOPK_MAX, S // 4)
    W = A_HEADS * LANES
    return pl.pallas_call(
        functools.partial(_dsa_prompt_kernel, tq=tq, topk=topk),
        grid=(B, nq),
        in_specs=[pl.BlockSpec((tq, W), lambda b, i: (b * nq + i, 0)),
                  pl.BlockSpec((S, LANES), lambda b, i: (b, 0)),
                  pl.BlockSpec((IDX_HEADS, tq), lambda b, i: (0, b * nq + i)),
                  pl.BlockSpec((tq, W), lambda b, i: (b * nq + i, 0)),
                  pl.BlockSpec((S, LANES), lambda b, i: (b, 0)),
                  pl.BlockSpec((1, LANES, S), lambda b, i: (b, 0, 0))],
        out_specs=pl.BlockSpec((tq, A_HEADS * A_HEAD_DIM), lambda b, i: (b * nq + i, 0)),
        out_shape=jax.ShapeDtypeStruct((B * S, A_HEADS * A_HEAD_DIM), BF16),
        compiler_params=_params("parallel", "arbitrary"), name="dsa_prompt",
    )(iq, ik, iw_t, aq, ak, av_t)


GDN_BETA_LANE = 8
GDN_DECAY_LANE = 16
GDN_TAIL = SUBLANES


def _softplus(x):
    return jnp.maximum(x, 0.0) + jnp.log1p(jnp.exp(-jnp.abs(x)))


def _silu(x):
    return x * jax.nn.sigmoid(x)


def _dot3(a, b, nt=False):
    f = _dot_nt if nt else _dot
    a1 = a.astype(BF16)
    a2 = (a - a1.astype(F32)).astype(BF16)
    b1 = b.astype(BF16)
    b2 = (b - b1.astype(F32)).astype(BF16)
    return f(a1, b1) + (f(a1, b2) + f(a2, b1))


def _gdn_kernel(xin_ref, z_ref, misc_ref, misct_ref, tail0_ref, s0_ref, cw_ref, alog_ref, dtb_ref,
                alogt_ref, dtbt_ref, ng_ref, o_ref, tail_ref, sout_ref, xp_ref, s_ref, *, C):
    c = pl.program_id(1)
    NP = B_HEADS // 2

    @pl.when(c == 0)
    def _():
        xp_ref[0:GDN_TAIL, :] = tail0_ref[0]
        s_ref[...] = s0_ref[0]

    @pl.when(c > 0)
    def _():
        xp_ref[0:GDN_TAIL, :] = xp_ref[C:C + GDN_TAIL, :]

    xp_ref[GDN_TAIL:GDN_TAIL + C, :] = xin_ref[...]
    tail_ref[0] = xp_ref[C:C + GDN_TAIL, :]
    base = GDN_TAIL - (CONV_WIDTH - 1)
    y = xp_ref[base:base + C, :] * cw_ref[0:1, :]
    for i in range(1, CONV_WIDTH):
        y = y + xp_ref[base + i:base + i + C, :] * cw_ref[i:i + 1, :]
    xc = _silu(y)

    misc = misc_ref[...]
    misct = misct_ref[0]
    beta = jax.nn.sigmoid(misc)
    g = -jnp.exp(alog_ref[...]) * _softplus(misc + dtb_ref[...])
    gt = -jnp.exp(alogt_ref[...]) * _softplus(misct + dtbt_ref[...])
    ti = lax.broadcasted_iota(I32, (C, C), 0)
    tj = lax.broadcasted_iota(I32, (C, C), 1)
    incl = tj <= ti
    strict = tj < ti
    eye = (ti == tj).astype(F32)
    g1, g2, g3 = _split3(g)
    low = incl.astype(BF16)
    gc = _dot(low, g1) + (_dot(low, g2) + _dot(low, g3))
    t1, t2, t3 = _split3(gt)
    upp = (ti <= tj).astype(BF16)
    gct = _dot(t1, upp) + (_dot(t2, upp) + _dot(t3, upp))

    lane = lax.broadcasted_iota(I32, (C, LANES), 1)
    first = lane < B_HEAD_DIM
    li = lax.broadcasted_iota(I32, (LANES, LANES), 0)
    lj = lax.broadcasted_iota(I32, (LANES, LANES), 1)
    blockdiag = (li < B_HEAD_DIM) == (lj < B_HEAD_DIM)
    ones_bd = blockdiag.astype(BF16)

    def pair(col0, col1):
        return jnp.where(first, col0, col1)

    for p in range(NP):
        sl = slice(p * LANES, (p + 1) * LANES)
        q2 = xc[:, sl]
        k2 = xc[:, GDN_QK + p * LANES:GDN_QK + (p + 1) * LANES]
        v2 = xc[:, 2 * GDN_QK + p * LANES:2 * GDN_QK + (p + 1) * LANES]
        q2 = q2 * lax.rsqrt(_dot_exact_rhs(q2 * q2, ones_bd) + EPS) * (B_HEAD_DIM ** -0.5)
        k2 = k2 * lax.rsqrt(_dot_exact_rhs(k2 * k2, ones_bd) + EPS)
        k2b = k2.astype(BF16)
        cols = {}
        for name, arr, off in (("b", beta, GDN_BETA_LANE), ("g", gc, GDN_DECAY_LANE)):
            for e in range(2):
                h = 2 * p + e
                cols[name, e] = arr[:, off + h:off + h + 1]
        glast = [cols["g", e][C - 1:C, :] for e in range(2)]
        beta2 = pair(cols["b", 0], cols["b", 1])
        gam2 = jnp.exp(pair(cols["g", 0], cols["g", 1]))
        kd2 = k2 * jnp.exp(pair(glast[0] - cols["g", 0], glast[1] - cols["g", 1]))
        gl2 = jnp.exp(jnp.where(first[0:1, :], glast[0], glast[1]))
        rhs = jnp.concatenate([v2 * beta2, k2 * (beta2 * gam2)], axis=1)
        sols, aqks = [], []
        for e in range(2):
            h = 2 * p + e
            sel = first if e == 0 else jnp.logical_not(first)
            grow = gct[GDN_DECAY_LANE + h:GDN_DECAY_LANE + h + 1, :]
            decay = jnp.exp(jnp.where(incl, cols["g", e] - grow, -jnp.inf))
            kk = _dot3(jnp.where(sel, k2, 0.0), k2, nt=True)
            a = jnp.where(strict, cols["b", e] * kk * decay, 0.0)
            tinv = eye - a
            xk = a
            span = 2
            while span < C:
                xk = _dot3(xk, xk)
                tinv = tinv + _dot3(tinv, xk)
                span *= 2
            sols.append(_dot3(tinv, rhs))
            aqks.append(_dot_nt(jnp.where(sel, q2, 0.0).astype(BF16), k2b) * decay)
        u2 = pair(sols[0][:, :LANES], sols[1][:, :LANES])
        w2 = pair(sols[0][:, LANES:], sols[1][:, LANES:])
        s_old = s_ref[p]
        sb = s_old.astype(BF16)
        delta = u2 - _dot(w2.astype(BF16), sb)
        db = delta.astype(BF16)
        o2 = _dot((q2 * gam2).astype(BF16), sb) + pair(_dot(aqks[0].astype(BF16), db),
                                                        _dot(aqks[1].astype(BF16), db))
        upd = lax.dot_general(kd2.astype(BF16), db, (((0,), (0,)), ((), ())),
                              preferred_element_type=F32)
        s_new = s_old * gl2 + jnp.where(blockdiag, upd, 0.0)
        s_ref[p] = s_new
        sout_ref[0, p] = s_new
        ms = _dot_exact_rhs(o2 * o2, ones_bd) * (1.0 / B_HEAD_DIM)
        o_ref[:, sl] = (o2 * lax.rsqrt(ms + EPS) * ng_ref[...] * _silu(z_ref[:, sl])).astype(o_ref.dtype)


def _gdn(xin, z, misc, tail0, s0, conv_w, a_log, dt_bias, norm_g, *, B, L):
    C = math.gcd(L, GDN_CHUNK)
    n = L // C
    M = B * L
    NP = B_HEADS // 2
    nrow = GDN_DECAY_LANE + B_HEADS
    misc_t = jnp.swapaxes(misc[:, :nrow].reshape(M // C, C, nrow), 1, 2)
    lane_vec = lambda v: jnp.zeros((1, LANES), F32).at[0, GDN_DECAY_LANE:nrow].set(v)
    col_vec = lambda v: jnp.zeros((nrow, 1), F32).at[GDN_DECAY_LANE:, 0].set(v)
    cw = jnp.pad(conv_w, ((0, SUBLANES - CONV_WIDTH), (0, 0)))
    ng = jnp.tile(norm_g.reshape(1, B_HEAD_DIM), (1, 2))
    full = lambda shape: pl.BlockSpec(shape, lambda b, c: (0,) * len(shape))
    return pl.pallas_call(
        functools.partial(_gdn_kernel, C=C),
        grid=(B, n),
        in_specs=[pl.BlockSpec((C, GDN_CONV_DIM), lambda b, c: (b * n + c, 0)),
                  pl.BlockSpec((C, GDN_QK), lambda b, c: (b * n + c, 0)),
                  pl.BlockSpec((C, LANES), lambda b, c: (b * n + c, 0)),
                  pl.BlockSpec((1, nrow, C), lambda b, c: (b * n + c, 0, 0)),
                  pl.BlockSpec((1, GDN_TAIL, GDN_CONV_DIM), lambda b, c: (b, 0, 0)),
                  pl.BlockSpec((1, NP, LANES, LANES), lambda b, c: (b, 0, 0, 0)),
                  full((SUBLANES, GDN_CONV_DIM)), full((1, LANES)), full((1, LANES)),
                  full((nrow, 1)), full((nrow, 1)), full((1, LANES))],
        out_specs=[pl.BlockSpec((C, GDN_QK), lambda b, c: (b * n + c, 0)),
                   pl.BlockSpec((1, GDN_TAIL, GDN_CONV_DIM), lambda b, c: (b, 0, 0)),
                   pl.BlockSpec((1, NP, LANES, LANES), lambda b, c: (b, 0, 0, 0))],
        out_shape=[jax.ShapeDtypeStruct((M, GDN_QK), BF16),
                   jax.ShapeDtypeStruct((B, GDN_TAIL, GDN_CONV_DIM), F32),
                   jax.ShapeDtypeStruct((B, NP, LANES, LANES), F32)],
        scratch_shapes=[pltpu.VMEM((C + GDN_TAIL, GDN_CONV_DIM), F32),
                        pltpu.VMEM((NP, LANES, LANES), F32)],
        compiler_params=_params("arbitrary", "arbitrary"), name="gdn",
    )(xin, z, misc, misc_t, tail0, s0, cw, lane_vec(a_log), lane_vec(dt_bias),
      col_vec(a_log), col_vec(dt_bias), ng)


def _state_to_blockdiag(s):
    B = s.shape[0]
    d = B_HEAD_DIM
    s = s.reshape(B, B_HEADS // 2, 2, d, d)
    z = jnp.zeros_like(s[:, :, 0])
    top = jnp.concatenate([s[:, :, 0], z], axis=-1)
    bot = jnp.concatenate([z, s[:, :, 1]], axis=-1)
    return jnp.concatenate([top, bot], axis=-2)


def _state_from_blockdiag(s):
    d = B_HEAD_DIM
    return jnp.stack([s[:, :, :d, :d], s[:, :, d:, d:]], axis=2).reshape(s.shape[0], B_HEADS, d, d)


PAGES_PER_STEP = 8


def _page_specs(block, layer, n):
    nd = len(block)

    def spec(p):
        return pl.BlockSpec(block, lambda b, j, pt, p=p: (pt[b, j * n + p], layer) + (0,) * (nd - 2))

    return [spec(p) for p in range(n)]


def _stack_heads(x, n_heads, width=LANES):
    return jnp.concatenate([x[:, h * width:(h + 1) * width] for h in range(n_heads)], axis=0)


def _pad_rows(x, rows):
    return jnp.concatenate([x, jnp.zeros((rows - x.shape[0], x.shape[1]), x.dtype)], axis=0)


def _dsa_s_score_kernel(pt_ref, iq_ref, misc_ref, *refs, n_pg, T):
    pages, o_ref = refs[:n_pg], refs[n_pg]
    q = _stack_heads(iq_ref[...], IDX_HEADS)[:, :IDX_DIM].astype(BF16)
    misc = misc_ref[...]
    w = jnp.concatenate([misc[:, h:h + 1] for h in range(IDX_HEADS)], axis=0)
    for p in range(n_pg):
        page_t = pages[p][0, 0].astype(BF16)
        r = jnp.maximum(_dot(q, page_t), 0.0) * w
        sc = r[0:T]
        for h in range(1, IDX_HEADS):
            sc = sc + r[h * T:(h + 1) * T]
        ps = page_t.shape[1]
        o_ref[0, :, p * ps:(p + 1) * ps] = sc


def _dsa_s_select_kernel(sp_ref, iq_ref, ikn_ref, misc_ref, bp_ref, bn_ref, *, T, topk):
    past = sp_ref.shape[2]
    misc = misc_ref[...]
    iq = iq_ref[...]
    ikn = _pad_rows(ikn_ref[...], LANES).astype(BF16)
    sn = jnp.zeros((T, LANES), F32)
    for h in range(IDX_HEADS):
        s = _dot_nt(iq[:, h * LANES:(h + 1) * LANES].astype(BF16), ikn)
        sn = sn + misc[:, h:h + 1] * jnp.maximum(s, 0.0)
    t_new = lax.broadcasted_iota(I32, (T, LANES), 0)
    n_new = lax.broadcasted_iota(I32, (T, LANES), 1)
    key_new = jnp.where(n_new <= t_new, _order_key(sn), INT_MIN)
    key = jnp.concatenate([_order_key(sp_ref[0]), key_new], axis=1)
    idx = lax.broadcasted_iota(I32, key.shape, 1)
    sel = _topk_mask(key, idx, topk, 1, (past + LANES - 1).bit_length())
    bias = jnp.where(sel, 0.0, NEG_BIG)
    bp_ref[0] = bias[:, :past]
    bn_ref[0] = jnp.where(n_new <= t_new, bias[:, past:], NEG_BIG)


def _online_softmax_step(s, v, m_ref, l_ref, acc_ref, v_transposed=False):
    m_old = m_ref[...]
    m_new = jnp.maximum(m_old, jnp.max(s, axis=-1, keepdims=True))
    corr = jnp.exp(m_old - m_new)
    p = jnp.exp(s - m_new)
    l_ref[...] = l_ref[...] * corr + jnp.sum(p, axis=-1, keepdims=True)
    pv = _dot_nt(p.astype(BF16), v) if v_transposed else _dot(p.astype(BF16), v)
    acc_ref[...] = acc_ref[...] * corr + pv
    m_ref[...] = m_new


def _dsa_s_attend_kernel(pt_ref, aq_ref, akn_ref, avn_ref, bp_ref, bn_ref, *refs, n_pg, T):
    kpages, vpages = refs[:n_pg], refs[n_pg:2 * n_pg]
    o_ref, q_ref, m_ref, l_ref, acc_ref = refs[2 * n_pg:]
    j = pl.program_id(1)
    scale = A_HEAD_DIM ** -0.5

    @pl.when(j == 0)
    def _():
        q_ref[...] = _stack_heads(aq_ref[...], A_HEADS).astype(BF16)
        m_ref[...] = jnp.full(m_ref.shape, NEG_BIG, F32)
        l_ref[...] = jnp.zeros(l_ref.shape, F32)
        acc_ref[...] = jnp.zeros(acc_ref.shape, F32)
        kn = _pad_rows(akn_ref[...], LANES).astype(BF16)
        vn = _pad_rows(avn_ref[...], LANES).astype(BF16)
        s = _dot_nt(q_ref[...], kn) * scale + jnp.concatenate([bn_ref[0]] * A_HEADS, axis=0)
        _online_softmax_step(s, vn, m_ref, l_ref, acc_ref)

    q = q_ref[...]
    for p in range(n_pg):
        kp_t = kpages[p][0, 0].astype(BF16)
        ps = kp_t.shape[1]
        bias = jnp.concatenate([bp_ref[0, :, p * ps:(p + 1) * ps]] * A_HEADS, axis=0)
        s = _dot(q, kp_t) * scale + bias
        _online_softmax_step(s, vpages[p][0, 0].astype(BF16), m_ref, l_ref, acc_ref, v_transposed=True)

    @pl.when(j == pl.num_programs(1) - 1)
    def _():
        o = acc_ref[...] / l_ref[...]
        lane = lax.broadcasted_iota(I32, (T, LANES), 1)
        rep = A_HEADS // A_KV_HEADS
        for hp in range(A_HEADS // 2):
            a = o[(2 * hp) * T:(2 * hp + 1) * T]
            b = o[(2 * hp + 1) * T:(2 * hp + 2) * T]
            if (2 * hp) // rep == 0:
                b = pltpu.roll(b, A_HEAD_DIM, 1)
            else:
                a = pltpu.roll(a, A_HEAD_DIM, 1)
            o_ref[:, hp * LANES:(hp + 1) * LANES] = jnp.where(lane < A_HEAD_DIM, a, b).astype(o_ref.dtype)


def _dsa_sample(iq, ikn, misc, aq, akn, avn, pool_k, pool_v, pool_idx, page_table, layer, *, Bd, T):
    n_pages = page_table.shape[1]
    page = pool_idx.shape[2]
    past = n_pages * page
    n_pg = math.gcd(PAGES_PER_STEP, n_pages)
    npg = n_pages // n_pg
    topk = min(TOPK_MAX, (past + T) // 4)
    W = A_HEADS * LANES
    kv_dim = A_KV_HEADS * A_HEAD_DIM
    pk = jnp.transpose(pool_k, (0, 1, 3, 4, 2)).reshape(pool_k.shape[:2] + (kv_dim, page))
    pv = jnp.transpose(pool_v, (0, 1, 3, 4, 2)).reshape(pool_v.shape[:2] + (kv_dim, page))
    pi = jnp.swapaxes(pool_idx, 2, 3)
    row = lambda w: pl.BlockSpec((T, w), lambda b, j, pt: (b, 0))

    scores = pl.pallas_call(
        functools.partial(_dsa_s_score_kernel, n_pg=n_pg, T=T),
        grid_spec=pltpu.PrefetchScalarGridSpec(
            num_scalar_prefetch=1, grid=(Bd, npg),
            in_specs=[row(W), row(LANES)] + _page_specs((1, 1, IDX_DIM, page), layer, n_pg),
            out_specs=pl.BlockSpec((1, T, n_pg * page), lambda b, j, pt: (b, 0, j))),
        out_shape=jax.ShapeDtypeStruct((Bd, T, past), F32),
        compiler_params=_params("parallel", "arbitrary"), name="dsa_s_score",
    )(page_table, iq, misc, *([pi] * n_pg))

    bias_p, bias_n = pl.pallas_call(
        functools.partial(_dsa_s_select_kernel, T=T, topk=topk),
        grid=(Bd,),
        in_specs=[pl.BlockSpec((1, T, past), lambda b: (b, 0, 0)),
                  pl.BlockSpec((T, W), lambda b: (b, 0)),
                  pl.BlockSpec((T, LANES), lambda b: (b, 0)),
                  pl.BlockSpec((T, LANES), lambda b: (b, 0))],
        out_specs=[pl.BlockSpec((1, T, past), lambda b: (b, 0, 0)),
                   pl.BlockSpec((1, T, LANES), lambda b: (b, 0, 0))],
        out_shape=[jax.ShapeDtypeStruct((Bd, T, past), F32),
                   jax.ShapeDtypeStruct((Bd, T, LANES), F32)],
        compiler_params=_params("parallel"), name="dsa_s_select",
    )(scores, iq, ikn, misc)

    return pl.pallas_call(
        functools.partial(_dsa_s_attend_kernel, n_pg=n_pg, T=T),
        grid_spec=pltpu.PrefetchScalarGridSpec(
            num_scalar_prefetch=1, grid=(Bd, npg),
            in_specs=[row(W), row(LANES), row(LANES),
                      pl.BlockSpec((1, T, n_pg * page), lambda b, j, pt: (b, 0, j)),
                      pl.BlockSpec((1, T, LANES), lambda b, j, pt: (b, 0, 0))]
            + _page_specs((1, 1, kv_dim, page), layer, n_pg) * 2,
            out_specs=pl.BlockSpec((T, A_HEADS * A_HEAD_DIM), lambda b, j, pt: (b, 0)),
            scratch_shapes=[pltpu.VMEM((A_HEADS * T, LANES), BF16),
                            pltpu.VMEM((A_HEADS * T, 1), F32),
                            pltpu.VMEM((A_HEADS * T, 1), F32),
                            pltpu.VMEM((A_HEADS * T, LANES), F32)]),
        out_shape=jax.ShapeDtypeStruct((Bd * T, A_HEADS * A_HEAD_DIM), BF16),
        compiler_params=_params("parallel", "arbitrary"), name="dsa_s_attend",
    )(page_table, aq, akn, avn, bias_p, bias_n, *([pk] * n_pg), *([pv] * n_pg))


def _head_matmul_kernel(x_ref, w_ref, o_ref):
    o_ref[0] = _dot(x_ref[...].astype(BF16), w_ref[0])


def _mla_s_attend_kernel(pt_ref, ql_ref, q_ref, latn_ref, krn_ref, *refs, n_pg, T):
    lpages, rpages = refs[:n_pg], refs[n_pg:2 * n_pg]
    o_ref, qlat_ref, qr_ref, m_ref, l_ref, acc_ref = refs[2 * n_pg:]
    j = pl.program_id(1)
    R = C_HEADS * T
    lo = C_NOPE

    @pl.when(j == 0)
    def _():
        qlat_ref[...] = ql_ref[...].reshape(R, C_KV_LORA).astype(BF16)
        q = q_ref[...]
        qr_ref[...] = jnp.concatenate(
            [q[:, h * LANES + lo:h * LANES + lo + C_ROPE] for h in range(C_HEADS)], axis=0).astype(BF16)
        m_ref[...] = jnp.full(m_ref.shape, NEG_BIG, F32)
        l_ref[...] = jnp.zeros(l_ref.shape, F32)
        acc_ref[...] = jnp.zeros(acc_ref.shape, F32)
        latn = _pad_rows(latn_ref[...], LANES).astype(BF16)
        krn = _pad_rows(krn_ref[...][:, lo:lo + C_ROPE], LANES).astype(BF16)
        s = (_dot_nt(qlat_ref[...], latn) + _dot_nt(qr_ref[...], krn)) * MLA_SCALE
        t_row = lax.broadcasted_iota(I32, (R, LANES), 0) % T
        n_key = lax.broadcasted_iota(I32, (R, LANES), 1)
        s = jnp.where(n_key <= t_row, s, NEG_BIG)
        _online_softmax_step(s, latn, m_ref, l_ref, acc_ref)

    ql = qlat_ref[...]
    qr = qr_ref[...]
    for p in range(n_pg):
        lat = lpages[p][0, 0].astype(BF16)
        kr_t = rpages[p][0, 0].astype(BF16)
        s = (_dot_nt(ql, lat) + _dot(qr, kr_t)) * MLA_SCALE
        _online_softmax_step(s, lat, m_ref, l_ref, acc_ref)

    @pl.when(j == pl.num_programs(1) - 1)
    def _():
        o_ref[...] = (acc_ref[...] / l_ref[...]).reshape(C_HEADS, T, C_KV_LORA)


def _pair_matmul_kernel(a_ref, b_ref, wa_ref, wb_ref, o_ref):
    o_ref[...] = (_dot(a_ref[0].astype(BF16), wa_ref[0])
                  + _dot(b_ref[0].astype(BF16), wb_ref[0])).astype(o_ref.dtype)


def _mla_sample(q, latn, krn, w_ukv, pool_lat, pool_kr, page_table, layer, *, Bd, T):
    Ms = Bd * T
    n_pages = page_table.shape[1]
    page = pool_lat.shape[2]
    n_pg = math.gcd(PAGES_PER_STEP, n_pages)
    npg = n_pages // n_pg
    w_uk_t = jnp.pad(jnp.transpose(w_ukv[..., :C_NOPE], (1, 2, 0)),
                     ((0, 0), (0, LANES - C_NOPE), (0, 0))).astype(BF16)
    wv = jnp.transpose(w_ukv[..., C_NOPE:], (1, 0, 2))
    even = (jnp.arange(C_HEADS) % 2 == 0)[:, None, None]
    z = jnp.zeros_like(wv)
    w_uv = jnp.concatenate([jnp.where(even, wv, z), jnp.where(even, z, wv)], axis=-1).astype(BF16)

    q_lat = pl.pallas_call(
        _head_matmul_kernel, grid=(C_HEADS,),
        in_specs=[pl.BlockSpec((Ms, LANES), lambda h: (0, h)),
                  pl.BlockSpec((1, LANES, C_KV_LORA), lambda h: (h, 0, 0))],
        out_specs=pl.BlockSpec((1, Ms, C_KV_LORA), lambda h: (h, 0, 0)),
        out_shape=jax.ShapeDtypeStruct((C_HEADS, Ms, C_KV_LORA), F32),
        compiler_params=_params("parallel"), name="mla_s_qlat",
    )(q, w_uk_t)

    R = C_HEADS * T
    row = lambda w: pl.BlockSpec((T, w), lambda b, j, pt: (b, 0))
    o_lat = pl.pallas_call(
        functools.partial(_mla_s_attend_kernel, n_pg=n_pg, T=T),
        grid_spec=pltpu.PrefetchScalarGridSpec(
            num_scalar_prefetch=1, grid=(Bd, npg),
            in_specs=[pl.BlockSpec((C_HEADS, T, C_KV_LORA), lambda b, j, pt: (0, b, 0)),
                      row(C_HEADS * LANES), row(C_KV_LORA), row(LANES)]
            + _page_specs((1, 1, page, C_KV_LORA), layer, n_pg)
            + _page_specs((1, 1, C_ROPE, page), layer, n_pg),
            out_specs=pl.BlockSpec((C_HEADS, T, C_KV_LORA), lambda b, j, pt: (0, b, 0)),
            scratch_shapes=[pltpu.VMEM((R, C_KV_LORA), BF16), pltpu.VMEM((R, C_ROPE), BF16),
                            pltpu.VMEM((R, 1), F32), pltpu.VMEM((R, 1), F32),
                            pltpu.VMEM((R, C_KV_LORA), F32)]),
        out_shape=jax.ShapeDtypeStruct((C_HEADS, Ms, C_KV_LORA), F32),
        compiler_params=_params("parallel", "arbitrary"), name="mla_s_attend",
    )(page_table, q_lat, q, latn, krn, *([pool_lat] * n_pg), *([jnp.swapaxes(pool_kr, 2, 3)] * n_pg))

    return pl.pallas_call(
        _pair_matmul_kernel, grid=(C_HEADS // 2,),
        in_specs=[pl.BlockSpec((1, Ms, C_KV_LORA), lambda h: (2 * h, 0, 0)),
                  pl.BlockSpec((1, Ms, C_KV_LORA), lambda h: (2 * h + 1, 0, 0)),
                  pl.BlockSpec((1, C_KV_LORA, LANES), lambda h: (2 * h, 0, 0)),
                  pl.BlockSpec((1, C_KV_LORA, LANES), lambda h: (2 * h + 1, 0, 0))],
        out_specs=pl.BlockSpec((Ms, LANES), lambda h: (0, h)),
        out_shape=jax.ShapeDtypeStruct((Ms, C_HEADS * C_V), BF16),
        compiler_params=_params("parallel"), name="mla_s_out",
    )(o_lat, o_lat, w_uv, w_uv)


def _mla_weights(w_down, w_uq, w_ukv):
    D = w_down.shape[0]
    kr_cols = w_down[:, C_Q_LORA + C_KV_LORA:]
    wd = jnp.concatenate([w_down[:, :C_Q_LORA + C_KV_LORA], jnp.zeros((D, C_NOPE), F32), kr_cols,
                          jnp.zeros((D, LANES - C_NOPE - C_ROPE), F32)], axis=1)
    wq = jnp.pad(w_uq, ((0, 0), (0, 0), (0, LANES - C_NOPE - C_ROPE))).reshape(C_Q_LORA, C_HEADS * LANES)
    wkv = w_ukv.reshape(C_KV_LORA, C_HEADS * LANES)
    return wd.astype(BF16), wq.astype(BF16), wkv.astype(BF16)


def _mla_project(x, g_mix, wd, wq, wkv, g_q, g_kv, rope, tm, qdtype):
    cq, ckv, kr = _proj(x, wd, ((C_Q_LORA, False, F32), (C_KV_LORA, False, F32), (LANES, True, F32)),
                        gain=g_mix, rope=rope, tm=tm, name="mla_down")
    (q,) = _proj(cq, wq, ((C_HEADS * LANES, True, qdtype),), gain=g_q, rope=rope, tm=tm, name="mla_uq")
    kv, lat = _proj(ckv, wkv, ((C_HEADS * LANES, False, BF16),), gain=g_kv, emit_norm=True, tm=tm,
                    name="mla_ukv")
    return q, kv, lat, kr


def _even_weights(w_in):
    D = w_in.shape[0]
    sizes = (A_HEADS * A_HEAD_DIM, A_KV_HEADS * A_HEAD_DIM, A_KV_HEADS * A_HEAD_DIM,
             IDX_HEADS * IDX_DIM, IDX_DIM, IDX_HEADS, GDN_QK, GDN_QK, GDN_QK, GDN_QK, B_HEADS, B_HEADS)
    parts, c = [], 0
    for s in sizes:
        parts.append(w_in[:, c:c + s])
        c += s
    aq, ak, av, iq, ik, iw, bq, bk, bv, bz, bb, ba = parts
    zero = jnp.zeros((D, A_HEADS, A_HEAD_DIM), F32)
    aq = aq.reshape(D, A_HEADS, A_HEAD_DIM)
    in_g0 = (jnp.arange(A_HEADS) < A_HEADS // A_KV_HEADS)[None, :, None]
    aq128 = jnp.concatenate([jnp.where(in_g0, aq, zero), jnp.where(in_g0, zero, aq)], axis=-1)
    iq128 = jnp.concatenate([iq.reshape(D, IDX_HEADS, IDX_DIM), zero], axis=-1)
    ik128 = jnp.pad(ik, ((0, 0), (0, LANES - IDX_DIM)))
    misc = jnp.pad(jnp.concatenate([iw, bb, ba], axis=1), ((0, 0), (0, LANES - IDX_HEADS - 2 * B_HEADS)))
    w = jnp.concatenate([aq128.reshape(D, -1), ak, iq128.reshape(D, -1), ik128, av, misc, bq, bk, bv, bz],
                        axis=1)
    misc_col = A_HEADS * LANES + LANES + IDX_HEADS * LANES + LANES + LANES
    scale = jnp.ones((w.shape[1],), F32).at[misc_col:misc_col + IDX_HEADS].set(IDX_W_SCALE)
    return w.astype(BF16), scale


def _even_groups(qdtype):
    return ((A_HEADS * LANES, True, qdtype), (LANES, True, F32), (IDX_HEADS * LANES, True, qdtype),
            (LANES, True, F32), (LANES, False, F32), (LANES, False, F32),
            (GDN_CONV_DIM, False, F32), (GDN_QK, False, F32))


def _tile_rows(t, rows):
    return jnp.tile(t, (rows // t.shape[0], 1))


def kernel(x_prompt, x_sample, cache_a_k, cache_a_v, cache_a_idx, state_b_ssm, state_b_conv,
           cache_c_latent, cache_c_krope, cache_mem_k, cache_mem_v, page_table, mem_prompt,
           g_mix, g_cross, g_mem, g_mlp, g_final,
           w_in_even, gdn_conv_w, gdn_a_log, gdn_dt_bias, gdn_norm_g, w_out_even,
           w_down_odd, g_q_lora, g_kv_lora, w_uq, w_ukv, w_out_odd,
           w_xq, w_xk, w_xv, w_xo, w_ff1, w_ff2):
    Bp, S, D = x_prompt.shape
    Bd, T, _ = x_sample.shape
    depth = g_mix.shape[0]
    Mp, Ms = Bp * S, Bd * T
    past_len = page_table.shape[1] * cache_a_k.shape[2]
    pos_p = jnp.arange(S, dtype=jnp.int32)
    pos_s = past_len + jnp.arange(T, dtype=jnp.int32)
    tm_p = min(256, S)
    tm_s = min(256, Ms)

    def rope_pair(dim, lo, hi):
        cp, sap, sbp, half = _rope_tables(pos_p, dim, lo, hi)
        cs, sas, sbs, _ = _rope_tables(pos_s, dim, lo, hi)
        return ((cp, sap, sbp, half, S // tm_p),
                (_tile_rows(cs, tm_s), _tile_rows(sas, tm_s), _tile_rows(sbs, tm_s), half, 1))

    rope_even_p, rope_even_s = rope_pair(A_HEAD_DIM, 0, LANES)
    rope_odd_p, rope_odd_s = rope_pair(C_ROPE, C_NOPE, C_NOPE + C_ROPE)

    xp = x_prompt.reshape(Mp, D)
    xs = x_sample.reshape(Ms, D)
    mem = mem_prompt.reshape(Bp * N_MEM, D)
    res = ((D, False, F32),)
    pa_k, pa_v, pa_i, pb_s, pb_c, pc_l, pc_r, pm_k, pm_v = [], [], [], [], [], [], [], [], []
    sa_k, sa_v, sa_i, sb_s, sb_c, sc_l, sc_r = [], [], [], [], [], [], []
    for li in range(depth):
        if li % 2 == 0:
            e = li // 2
            w_even, cscale = _even_weights(w_in_even[e])
            w_out = w_out_even[e].astype(BF16)
            gdn_w = (gdn_conv_w[e], gdn_a_log[e], gdn_dt_bias[e], gdn_norm_g[e])
            aq, ak, iq, ik, av, misc, conv_in, bz = _proj(
                xp, w_even, _even_groups(BF16), gain=g_mix[li], rope=rope_even_p, colscale=cscale,
                tm=tm_p, name="even_in")
            a_out = _dsa_prompt(iq, ik, misc[:, :IDX_HEADS].T, aq, ak,
                                jnp.swapaxes(av.reshape(Bp, S, LANES), 1, 2), B=Bp, S=S)
            b_out, tail, sbd = _gdn(
                conv_in, bz, misc, jnp.zeros((Bp, GDN_TAIL, GDN_CONV_DIM), F32),
                jnp.zeros((Bp, B_HEADS // 2, LANES, LANES), F32), *gdn_w, B=Bp, L=S)
            (xp,) = _proj(jnp.concatenate([a_out, b_out], axis=1), w_out, res, residual=xp, tm=tm_p,
                          name="even_out")
            pa_k.append(ak.reshape(Bp, S, A_KV_HEADS, A_HEAD_DIM))
            pa_v.append(av.reshape(Bp, S, A_KV_HEADS, A_HEAD_DIM))
            pa_i.append(ik[:, :IDX_DIM].reshape(Bp, S, IDX_DIM))
            pb_s.append(_state_from_blockdiag(sbd))
            pb_c.append(tail[:, GDN_TAIL - (CONV_WIDTH - 1):])
            aq, ak, iq, ik, av, misc, conv_in, bz = _proj(
                xs, w_even, _even_groups(F32), gain=g_mix[li], rope=rope_even_s, colscale=cscale,
                tm=tm_s, name="even_in_s")
            a_out = _dsa_sample(iq, ik, misc, aq, ak, av, cache_a_k, cache_a_v, cache_a_idx,
                                page_table, e, Bd=Bd, T=T)
            tail0 = jnp.pad(state_b_conv[e], ((0, 0), (GDN_TAIL - (CONV_WIDTH - 1), 0), (0, 0)))
            b_out, tail, sbd = _gdn(conv_in, bz, misc, tail0, _state_to_blockdiag(state_b_ssm[e]),
                                    *gdn_w, B=Bd, L=T)
            (xs,) = _proj(jnp.concatenate([a_out, b_out], axis=1), w_out, res, residual=xs, tm=tm_s,
                          name="even_out_s")
            sa_k.append(ak.reshape(Bd, T, A_KV_HEADS, A_HEAD_DIM))
            sa_v.append(av.reshape(Bd, T, A_KV_HEADS, A_HEAD_DIM))
            sa_i.append(ik[:, :IDX_DIM].reshape(Bd, T, IDX_DIM))
            sb_s.append(_state_from_blockdiag(sbd).astype(state_b_ssm.dtype))
            sb_c.append(tail[:, GDN_TAIL - (CONV_WIDTH - 1):])
        else:
            o = li // 2
            wd, wq, wkv = _mla_weights(w_down_odd[o], w_uq[o], w_ukv[o])
            w_out = w_out_odd[o].astype(BF16)
            q, kv, lat, kr = _mla_project(xp, g_mix[li], wd, wq, wkv, g_q_lora[o], g_kv_lora[o],
                                          rope_odd_p, tm_p, BF16)
            (xp,) = _proj(_mla_prompt(q, kv, kr, B=Bp, S=S), w_out, res, residual=xp, tm=tm_p,
                          name="odd_out")
            pc_l.append(lat.reshape(Bp, S, C_KV_LORA))
            pc_r.append(kr[:, C_NOPE:C_NOPE + C_ROPE].reshape(Bp, S, C_ROPE))
            q, kv, lat, kr = _mla_project(xs, g_mix[li], wd, wq, wkv, g_q_lora[o], g_kv_lora[o],
                                          rope_odd_s, tm_s, F32)
            a_out = _mla_sample(q, lat, kr, w_ukv[o], cache_c_latent, cache_c_krope, page_table, o,
                                Bd=Bd, T=T)
            (xs,) = _proj(a_out, w_out, res, residual=xs, tm=tm_s, name="odd_out_s")
            sc_l.append(lat.reshape(Bd, T, C_KV_LORA))
            sc_r.append(kr[:, C_NOPE:C_NOPE + C_ROPE].reshape(Bd, T, C_ROPE))
        w_kv = jnp.concatenate([w_xk[li], w_xv[li]], axis=1).astype(BF16)
        mk, mv = _proj(mem, w_kv, ((X_WIDTH, False, F32), (X_WIDTH, False, F32)), gain=g_mem[li],
                       tm=min(256, mem.shape[0]), name="mem_kv")
        pm_k.append(mk.reshape(Bp, N_MEM, X_HEADS, X_HEAD_DIM))
        pm_v.append(mv.reshape(Bp, N_MEM, X_HEADS, X_HEAD_DIM))
        wxq, wxo = w_xq[li].astype(BF16), w_xo[li].astype(BF16)
        xp = _xattn(xp, g_cross[li], wxq, mk.reshape(Bp, N_MEM, X_WIDTH), mv.reshape(Bp, N_MEM, X_WIDTH),
                    wxo, rows_per_batch=S, tm=tm_p)
        xs = _xattn(xs, g_cross[li], wxq, cache_mem_k[li].reshape(Bd, N_MEM, X_WIDTH),
                    cache_mem_v[li].reshape(Bd, N_MEM, X_WIDTH), wxo, rows_per_batch=T, tm=T)
        w1, w2 = w_ff1[li].astype(BF16), w_ff2[li].astype(BF16)
        last = li == depth - 1
        xp = _mlp(xp, g_mlp[li], w1, w2, g_final, final_norm=last)
        xs = _mlp(xs, g_mlp[li], w1, w2, g_final, final_norm=last)
    return (xp.reshape(Bp, S, D), xs.reshape(Bd, T, D),
            jnp.stack(pa_k, axis=1), jnp.stack(pa_v, axis=1), jnp.stack(pa_i, axis=1),
            jnp.stack(pb_s, axis=0), jnp.stack(pb_c, axis=0),
            jnp.stack(pc_l, axis=1), jnp.stack(pc_r, axis=1),
            jnp.stack(pm_k, axis=0), jnp.stack(pm_v, axis=0),
            jnp.stack(sa_k, axis=1), jnp.stack(sa_v, axis=1), jnp.stack(sa_i, axis=1),
            jnp.stack(sb_s, axis=0), jnp.stack(sb_c, axis=0),
            jnp.stack(sc_l, axis=1), jnp.stack(sc_r, axis=1))
```

```python
import functools
import math

import jax
import jax.numpy as jnp
from jax import lax
from jax.experimental import pallas as pl
from jax.experimental.pallas import tpu as pltpu

F32 = jnp.float32
BF16 = jnp.bfloat16
I32 = jnp.int32

LANES = 128
SUBLANES = 8
VMEM_LIMIT = 56 * 1024 * 1024

EPS = 1e-6
ROPE_THETA = 10000.0
N_MEM = 256
TOPK_MAX = 256

A_HEADS = 8
A_KV_HEADS = 2
A_HEAD_DIM = 64
IDX_HEADS = 8
IDX_DIM = 64
IDX_W_SCALE = (IDX_HEADS * IDX_DIM) ** -0.5

B_HEADS = 8
B_HEAD_DIM = 64
CONV_WIDTH = 4
GDN_CHUNK = 64
GDN_QK = B_HEADS * B_HEAD_DIM
GDN_CONV_DIM = 3 * GDN_QK

C_HEADS = 16
C_NOPE = 64
C_ROPE = 32
C_V = 64
C_Q_LORA = 384
C_KV_LORA = 256
MLA_SCALE = (C_NOPE + C_ROPE) ** -0.5

X_HEADS = 4
X_HEAD_DIM = 128
X_WIDTH = X_HEADS * X_HEAD_DIM
X_SCALE = X_HEAD_DIM ** -0.5

INT_MIN = -2 ** 31
NEG_BIG = -1e30


def _params(*sem):
    return pltpu.CompilerParams(dimension_semantics=sem, vmem_limit_bytes=VMEM_LIMIT)


def _rms(x, g):
    return x * lax.rsqrt(jnp.mean(x * x, axis=-1, keepdims=True) + EPS) * g


def _dot(a, b):
    return jnp.dot(a, b, preferred_element_type=F32)


def _dot_nt(a, b):
    return lax.dot_general(a, b, (((1,), (1,)), ((), ())), preferred_element_type=F32)


def _split3(a):
    a1 = a.astype(BF16)
    r = a - a1.astype(F32)
    a2 = r.astype(BF16)
    a3 = (r - a2.astype(F32)).astype(BF16)
    return a1, a2, a3


def _dot_hi(a, b, nt=False):
    f = _dot_nt if nt else _dot
    a1, a2, a3 = _split3(a)
    b1, b2, b3 = _split3(b)
    small = f(a1, b3) + f(a3, b1) + f(a2, b2)
    mid = f(a1, b2) + f(a2, b1)
    return f(a1, b1) + (mid + small)


def _dot_exact_rhs(a, b_bf16):
    a1, a2, a3 = _split3(a)
    return _dot(a1, b_bf16) + (_dot(a2, b_bf16) + _dot(a3, b_bf16))


def _proj_kernel(*refs, groups, has_gain, has_rope, has_scale, has_res, emit_norm, shift):
    it = iter(refs)
    x_ref = next(it)
    g_ref = next(it) if has_gain else None
    w_ref = next(it)
    if has_rope:
        c_ref, sa_ref, sb_ref = next(it), next(it), next(it)
    s_ref = next(it) if has_scale else None
    r_ref = next(it) if has_res else None
    outs = list(it)
    x = x_ref[...].astype(F32)
    if has_gain:
        x = _rms(x, g_ref[...])
    if emit_norm:
        outs[-1][...] = x
    xb = x.astype(BF16)
    col = 0
    for (width, rope, _), o_ref in zip(groups, outs):
        y = _dot(xb, w_ref[:, col:col + width])
        if has_scale:
            y = y * s_ref[:, col:col + width]
        if rope:
            n = width // LANES
            c = jnp.concatenate([c_ref[...]] * n, axis=1)
            sa = jnp.concatenate([sa_ref[...]] * n, axis=1)
            sb = jnp.concatenate([sb_ref[...]] * n, axis=1)
            y = y * c + pltpu.roll(y, shift, 1) * sa + pltpu.roll(y, width - shift, 1) * sb
        if has_res:
            y = y + r_ref[...]
        o_ref[...] = y.astype(o_ref.dtype)
        col += width


def _proj(x, w, groups, *, gain=None, rope=None, colscale=None, residual=None,
          emit_norm=False, tm=256, name="proj"):
    M, K = x.shape
    N = w.shape[1]
    assert sum(g[0] for g in groups) == N and M % tm == 0
    args, specs = [x], [pl.BlockSpec((tm, K), lambda i: (i, 0))]
    if gain is not None:
        args.append(gain.reshape(1, K).astype(F32))
        specs.append(pl.BlockSpec((1, K), lambda i: (0, 0)))
    args.append(w)
    specs.append(pl.BlockSpec((K, N), lambda i: (0, 0)))
    shift = 0
    if rope is not None:
        c, sa, sb, shift, nblk = rope
        for t in (c, sa, sb):
            args.append(t)
            specs.append(pl.BlockSpec((tm, LANES), lambda i, nblk=nblk: (i % nblk, 0)))
    if colscale is not None:
        args.append(colscale.reshape(1, N).astype(F32))
        specs.append(pl.BlockSpec((1, N), lambda i: (0, 0)))
    if residual is not None:
        assert len(groups) == 1
        args.append(residual)
        specs.append(pl.BlockSpec((tm, N), lambda i: (i, 0)))
    out_shape = [jax.ShapeDtypeStruct((M, g[0]), g[2]) for g in groups]
    out_specs = [pl.BlockSpec((tm, g[0]), lambda i: (i, 0)) for g in groups]
    if emit_norm:
        out_shape.append(jax.ShapeDtypeStruct((M, K), F32))
        out_specs.append(pl.BlockSpec((tm, K), lambda i: (i, 0)))
    kern = functools.partial(
        _proj_kernel, groups=tuple(groups), has_gain=gain is not None,
        has_rope=rope is not None, has_scale=colscale is not None,
        has_res=residual is not None, emit_norm=emit_norm, shift=shift)
    return pl.pallas_call(
        kern, grid=(M // tm,), in_specs=specs, out_specs=out_specs,
        out_shape=out_shape, compiler_params=_params("parallel"), name=name)(*args)


def _mlp_kernel(x_ref, g_ref, w1_ref, w2_ref, gf_ref, o_ref, hn_ref, acc_ref, *, final_norm):
    j = pl.program_id(1)

    @pl.when(j == 0)
    def _():
        hn_ref[...] = _rms(x_ref[...], g_ref[...]).astype(BF16)
        acc_ref[...] = jnp.zeros_like(acc_ref)

    a = _dot(hn_ref[...], w1_ref[...])
    a = jnp.square(jnp.maximum(a, 0.0)).astype(BF16)
    acc_ref[...] += _dot(a, w2_ref[...])

    @pl.when(j == pl.num_programs(1) - 1)
    def _():
        y = x_ref[...] + acc_ref[...]
        if final_norm:
            y = _rms(y, gf_ref[...])
        o_ref[...] = y


def _mlp(x, g, w1, w2, g_final, *, final_norm, tm=512, tf=1024):
    M, D = x.shape
    F = w1.shape[1]
    tm = min(tm, M)
    assert M % tm == 0 and F % tf == 0
    return pl.pallas_call(
        functools.partial(_mlp_kernel, final_norm=final_norm),
        grid=(M // tm, F // tf),
        in_specs=[pl.BlockSpec((tm, D), lambda i, j: (i, 0)),
                  pl.BlockSpec((1, D), lambda i, j: (0, 0)),
                  pl.BlockSpec((D, tf), lambda i, j: (0, j)),
                  pl.BlockSpec((tf, D), lambda i, j: (j, 0)),
                  pl.BlockSpec((1, D), lambda i, j: (0, 0))],
        out_specs=pl.BlockSpec((tm, D), lambda i, j: (i, 0)),
        out_shape=jax.ShapeDtypeStruct((M, D), F32),
        scratch_shapes=[pltpu.VMEM((tm, D), BF16), pltpu.VMEM((tm, D), F32)],
        compiler_params=_params("parallel", "arbitrary"), name="mlp",
    )(x, g.reshape(1, D), w1, w2, g_final.reshape(1, D))


def _xattn_kernel(x_ref, g_ref, wq_ref, mk_ref, mv_ref, wo_ref, o_ref):
    x = x_ref[...]
    q = _dot(_rms(x, g_ref[...]).astype(BF16), wq_ref[...])
    mk = mk_ref[0].astype(BF16)
    mv = mv_ref[0].astype(BF16)
    heads = []
    for h in range(X_HEADS):
        sl = slice(h * X_HEAD_DIM, (h + 1) * X_HEAD_DIM)
        s = _dot_nt(q[:, sl].astype(BF16), mk[:, sl]) * X_SCALE
        s = s - jnp.max(s, axis=-1, keepdims=True)
        p = jnp.exp(s)
        p = p / jnp.sum(p, axis=-1, keepdims=True)
        heads.append(_dot(p.astype(BF16), mv[:, sl]))
    o = jnp.concatenate(heads, axis=1).astype(BF16)
    o_ref[...] = x + _dot(o, wo_ref[...])


def _xattn(x, g, wq, mk, mv, wo, *, rows_per_batch, tm):
    M, D = x.shape
    assert rows_per_batch % tm == 0
    per = rows_per_batch // tm
    nm = mk.shape[1]
    return pl.pallas_call(
        _xattn_kernel, grid=(M // tm,),
        in_specs=[pl.BlockSpec((tm, D), lambda i: (i, 0)),
                  pl.BlockSpec((1, D), lambda i: (0, 0)),
                  pl.BlockSpec((D, X_WIDTH), lambda i: (0, 0)),
                  pl.BlockSpec((1, nm, X_WIDTH), lambda i: (i // per, 0, 0)),
                  pl.BlockSpec((1, nm, X_WIDTH), lambda i: (i // per, 0, 0)),
                  pl.BlockSpec((X_WIDTH, D), lambda i: (0, 0))],
        out_specs=pl.BlockSpec((tm, D), lambda i: (i, 0)),
        out_shape=jax.ShapeDtypeStruct((M, D), F32),
        compiler_params=_params("parallel"), name="xattn",
    )(x, g.reshape(1, D), wq, mk, mv, wo)


def _rope_tables(pos, dim, lane_lo, lane_hi):
    half = dim // 2
    lane = jnp.arange(LANES)
    j = (lane - lane_lo) % dim
    inside = (lane >= lane_lo) & (lane < lane_hi)
    inv = jnp.exp(-math.log(ROPE_THETA) * (j % half).astype(F32) / half)
    ang = pos.astype(F32)[:, None] * inv[None, :]
    cos, sin = jnp.cos(ang), jnp.sin(ang)
    c = jnp.where(inside[None, :], cos, 1.0)
    sa = jnp.where((inside & (j >= half))[None, :], sin, 0.0)
    sb = jnp.where((inside & (j < half))[None, :], -sin, 0.0)
    return c, sa, sb, half


def _mla_prompt_kernel(q_ref, kv_ref, kr_ref, o_ref, *, tq, tk):
    i = pl.program_id(2)
    n_chunks = (i * tq + tq + tk - 1) // tk
    lane = lax.broadcasted_iota(I32, (tk, LANES), 1)
    qpos = i * tq + lax.broadcasted_iota(I32, (tq, tk), 0)
    kofs = lax.broadcasted_iota(I32, (tq, tk), 1)
    qs = [q_ref[:, h * LANES:(h + 1) * LANES] for h in range(2)]

    def body(c, carry):
        start = pl.multiple_of(c * tk, tk)
        kr = kr_ref[pl.ds(start, tk), :]
        causal = kofs + c * tk <= qpos
        new = []
        for h in range(2):
            m, l, acc = carry[h]
            kv = kv_ref[pl.ds(start, tk), h * LANES:(h + 1) * LANES]
            kcat = jnp.where(lane < C_NOPE, kv, kr.astype(kv.dtype))
            s = jnp.where(causal, _dot_nt(qs[h], kcat) * MLA_SCALE, NEG_BIG)
            m_new = jnp.maximum(m, jnp.max(s, axis=-1, keepdims=True))
            corr = jnp.exp(m - m_new)
            p = jnp.exp(s - m_new)
            new.append((m_new, l * corr + jnp.sum(p, axis=-1, keepdims=True),
                        acc * corr + _dot(p.astype(BF16), kv)))
        return tuple(new)

    one = (jnp.full((tq, 1), NEG_BIG, F32), jnp.zeros((tq, 1), F32), jnp.zeros((tq, LANES), F32))
    res = lax.fori_loop(0, n_chunks, body, (one, one))
    outs = [acc / l for _, l, acc in res]
    lane_o = lax.broadcasted_iota(I32, (tq, LANES), 1)
    o_ref[...] = jnp.where(lane_o < C_V, pltpu.roll(outs[0], C_V, 1), outs[1]).astype(o_ref.dtype)


def _mla_prompt(q, kv, kr, *, B, S, tq=256, tk=512):
    tq = min(tq, S)
    tk = min(tk, S)
    nq = S // tq
    return pl.pallas_call(
        functools.partial(_mla_prompt_kernel, tq=tq, tk=tk),
        grid=(B, C_HEADS // 2, nq),
        in_specs=[pl.BlockSpec((tq, 2 * LANES), lambda b, h, i: (b * nq + i, h)),
                  pl.BlockSpec((S, 2 * LANES), lambda b, h, i: (b, h)),
                  pl.BlockSpec((S, LANES), lambda b, h, i: (b, 0))],
        out_specs=pl.BlockSpec((tq, LANES), lambda b, h, i: (b * nq + i, h)),
        out_shape=jax.ShapeDtypeStruct((B * S, C_HEADS * C_V), BF16),
        compiler_params=_params("parallel", "parallel", "arbitrary"), name="mla_prompt",
    )(q, kv, kr)


def _order_key(sc):
    sc = jnp.where(sc == 0.0, 0.0, sc)
    bits = lax.bitcast_convert_type(sc, I32)
    return bits ^ ((bits >> 31) & 0x7FFFFFFF)


def _topk_search(count, k, n_idx_bits):
    kf = float(k)
    t0 = jnp.where(count(lambda key, idx: key >= 0) >= kf, 0, INT_MIN).astype(I32)

    def vbody(n, t):
        cand = t | jnp.left_shift(jnp.int32(1), 30 - n)
        return jnp.where(count(lambda key, idx: key >= cand) >= kf, cand, t)

    t = lax.fori_loop(0, 31, vbody, t0)
    need = kf - count(lambda key, idx: key > t)
    n_eq = count(lambda key, idx: key == t)

    def search():
        def ibody(n, x):
            cand = x | jnp.left_shift(jnp.int32(1), n_idx_bits - 1 - n)
            return jnp.where(count(lambda key, idx: (key == t) & (idx < cand)) < need, cand, x)

        return lax.fori_loop(0, n_idx_bits, ibody, jnp.zeros_like(t))

    tied = jnp.max(jnp.where((n_eq > need) & (t > INT_MIN), 1.0, 0.0)) > 0.5
    x = lax.cond(tied, search, lambda: jnp.full_like(t, (1 << n_idx_bits) - 1))
    return t, x


def _topk_mask(key, idx, k, axis, n_idx_bits):
    def count(pred):
        return jnp.sum(pred(key, idx).astype(F32), axis=axis, keepdims=True)

    t, x = _topk_search(count, k, n_idx_bits)
    return (key > t) | ((key == t) & (idx <= x))


def _dsa_prompt_kernel(iq_ref, ik_ref, iw_ref, aq_ref, ak_ref, avt_ref, o_ref,
                       key_ref, m_ref, l_ref, acc_ref, *, tq, topk):
    i = pl.program_id(1)
    S = ik_ref.shape[0]
    n_chunks = i + 1
    kofs = lax.broadcasted_iota(I32, (tq, tq), 0)
    qidx = i * tq + lax.broadcasted_iota(I32, (tq, tq), 1)

    def chunk_start(c):
        return pl.multiple_of(c * tq, tq)

    def score_chunk(c, carry):
        ik = ik_ref[pl.ds(chunk_start(c), tq), :].astype(BF16)
        sc = jnp.zeros((tq, tq), F32)
        for h in range(IDX_HEADS):
            s = _dot_nt(ik, iq_ref[:, h * LANES:(h + 1) * LANES])
            sc = sc + iw_ref[h:h + 1, :] * jnp.maximum(s, 0.0)
        key_ref[pl.ds(chunk_start(c), tq), :] = jnp.where(kofs + c * tq <= qidx, _order_key(sc), INT_MIN)
        return carry

    lax.fori_loop(0, n_chunks, score_chunk, 0)

    def count(pred):
        def body(c, acc):
            hit = pred(key_ref[pl.ds(chunk_start(c), tq), :], kofs + c * tq)
            return acc + jnp.sum(hit.astype(F32).reshape(tq // SUBLANES, SUBLANES, tq), axis=0)

        part = lax.fori_loop(0, n_chunks, body, jnp.zeros((SUBLANES, tq), F32))
        return jnp.sum(part, axis=0, keepdims=True)

    t, x = _topk_search(count, topk, max(1, (S - 1).bit_length()))

    m_ref[...] = jnp.full(m_ref.shape, NEG_BIG, F32)
    l_ref[...] = jnp.zeros(l_ref.shape, F32)
    acc_ref[...] = jnp.zeros(acc_ref.shape, F32)
    rep = A_HEADS // A_KV_HEADS

    def attend_chunk(c, carry):
        start = chunk_start(c)
        key = key_ref[pl.ds(start, tq), :]
        kidx = kofs + c * tq
        sel = ((key > t) | ((key == t) & (kidx <= x))) & (kidx <= qidx)
        ak = ak_ref[pl.ds(start, tq), :].astype(BF16)
        avt = avt_ref[0, :, pl.ds(start, tq)].astype(BF16)
        heads = range(A_HEADS)
        scale = A_HEAD_DIM ** -0.5
        s = [jnp.where(sel, _dot_nt(ak, aq_ref[:, h * LANES:(h + 1) * LANES]) * scale, NEG_BIG)
             for h in heads]
        m_old = m_ref[...]
        m_new = jnp.maximum(m_old, jnp.concatenate([jnp.max(s[h], axis=0, keepdims=True) for h in heads],
                                                   axis=0))
        corr = jnp.exp(m_old - m_new)
        p = [jnp.exp(s[h] - m_new[h:h + 1, :]) for h in heads]
        l_ref[...] = l_ref[...] * corr + jnp.concatenate(
            [jnp.sum(p[h], axis=0, keepdims=True) for h in heads], axis=0)
        pv = [_dot(avt[(h // rep) * A_HEAD_DIM:(h // rep + 1) * A_HEAD_DIM], p[h].astype(BF16))
              for h in heads]
        for h in heads:
            acc_ref[h] = acc_ref[h] * corr[h:h + 1, :] + pv[h]
        m_ref[...] = m_new
        return carry

    lax.fori_loop(0, n_chunks, attend_chunk, 0)
    outs = [acc_ref[h] / l_ref[h:h + 1, :] for h in range(A_HEADS)]
    o_ref[...] = jnp.concatenate(outs, axis=0).T.astype(o_ref.dtype)


def _dsa_prompt(iq, ik, iw_t, aq, ak, av_t, *, B, S, tq=256):
    tq = min(tq, S)
    nq = S // tq
    topk = min(TOPK_MAX, S // 4)
    W = A_HEADS * LANES
    return pl.pallas_call(
        functools.partial(_dsa_prompt_kernel, tq=tq, topk=topk),
        grid=(B, nq),
        scratch_shapes=[pltpu.VMEM((S, tq), I32), pltpu.VMEM((A_HEADS, tq), F32),
                        pltpu.VMEM((A_HEADS, tq), F32), pltpu.VMEM((A_HEADS, A_HEAD_DIM, tq), F32)],
        in_specs=[pl.BlockSpec((tq, W), lambda b, i: (b * nq + i, 0)),
                  pl.BlockSpec((S, LANES), lambda b, i: (b, 0)),
                  pl.BlockSpec((IDX_HEADS, tq), lambda b, i: (0, b * nq + i)),
                  pl.BlockSpec((tq, W), lambda b, i: (b * nq + i, 0)),
                  pl.BlockSpec((S, LANES), lambda b, i: (b, 0)),
                  pl.BlockSpec((1, LANES, S), lambda b, i: (b, 0, 0))],
        out_specs=pl.BlockSpec((tq, A_HEADS * A_HEAD_DIM), lambda b, i: (b * nq + i, 0)),
        out_shape=jax.ShapeDtypeStruct((B * S, A_HEADS * A_HEAD_DIM), BF16),
        compiler_params=_params("parallel", "arbitrary"), name="dsa_prompt",
    )(iq, ik, iw_t, aq, ak, av_t)


GDN_BETA_LANE = 8
GDN_DECAY_LANE = 16
GDN_TAIL = SUBLANES


def _softplus(x):
    return jnp.maximum(x, 0.0) + jnp.log1p(jnp.exp(-jnp.abs(x)))


def _silu(x):
    return x * jax.nn.sigmoid(x)


def _dot3(a, b, nt=False):
    f = _dot_nt if nt else _dot
    a1 = a.astype(BF16)
    a2 = (a - a1.astype(F32)).astype(BF16)
    b1 = b.astype(BF16)
    b2 = (b - b1.astype(F32)).astype(BF16)
    return f(a1, b1) + (f(a1, b2) + f(a2, b1))


def _gdn_kernel(xin_ref, z_ref, misc_ref, misct_ref, tail0_ref, s0_ref, cw_ref, alog_ref, dtb_ref,
                alogt_ref, dtbt_ref, ng_ref, o_ref, tail_ref, sout_ref, xp_ref, s_ref, *, C):
    c = pl.program_id(1)
    NP = B_HEADS // 2

    @pl.when(c == 0)
    def _():
        xp_ref[0:GDN_TAIL, :] = tail0_ref[0]
        s_ref[...] = s0_ref[0]

    @pl.when(c > 0)
    def _():
        xp_ref[0:GDN_TAIL, :] = xp_ref[C:C + GDN_TAIL, :]

    xp_ref[GDN_TAIL:GDN_TAIL + C, :] = xin_ref[...]
    tail_ref[0] = xp_ref[C:C + GDN_TAIL, :]
    base = GDN_TAIL - (CONV_WIDTH - 1)
    y = xp_ref[base:base + C, :] * cw_ref[0:1, :]
    for i in range(1, CONV_WIDTH):
        y = y + xp_ref[base + i:base + i + C, :] * cw_ref[i:i + 1, :]
    xc = _silu(y)

    misc = misc_ref[...]
    misct = misct_ref[0]
    beta = jax.nn.sigmoid(misc)
    g = -jnp.exp(alog_ref[...]) * _softplus(misc + dtb_ref[...])
    gt = -jnp.exp(alogt_ref[...]) * _softplus(misct + dtbt_ref[...])
    ti = lax.broadcasted_iota(I32, (C, C), 0)
    tj = lax.broadcasted_iota(I32, (C, C), 1)
    g1, g2, g3 = _split3(g)
    low = (tj <= ti).astype(BF16)
    gc = _dot(low, g1) + (_dot(low, g2) + _dot(low, g3))
    t1, t2, t3 = _split3(gt)
    upp = (ti <= tj).astype(BF16)
    gct = _dot(t1, upp) + (_dot(t2, upp) + _dot(t3, upp))

    first = lax.broadcasted_iota(I32, (C, LANES), 1) < B_HEAD_DIM
    li = lax.broadcasted_iota(I32, (LANES, LANES), 0)
    lj = lax.broadcasted_iota(I32, (LANES, LANES), 1)
    blockdiag = (li < B_HEAD_DIM) == (lj < B_HEAD_DIM)
    ones_bd = blockdiag.astype(BF16)
    own_lanes = ((lax.broadcasted_iota(I32, (2 * C, LANES), 0) < C)
                 == (lax.broadcasted_iota(I32, (2 * C, LANES), 1) < B_HEAD_DIM))
    ri = lax.broadcasted_iota(I32, (2 * C, 2 * C), 0)
    rj = lax.broadcasted_iota(I32, (2 * C, 2 * C), 1)
    same_head = (ri < C) == (rj < C)
    incl = same_head & (rj <= ri)
    strict = same_head & (rj < ri)
    eye = (ri == rj).astype(F32)

    def pair(col0, col1):
        return jnp.where(first, col0, col1)

    def halves(x):
        return pair(x[:C, :LANES], x[C:, :LANES])

    def twice(x):
        return jnp.concatenate([x, x], axis=0)

    st = []
    for p in range(NP):
        sl = slice(p * LANES, (p + 1) * LANES)
        q2 = xc[:, sl]
        k2 = xc[:, GDN_QK + p * LANES:GDN_QK + (p + 1) * LANES]
        v2 = xc[:, 2 * GDN_QK + p * LANES:2 * GDN_QK + (p + 1) * LANES]
        q2 = q2 * lax.rsqrt(_dot_exact_rhs(q2 * q2, ones_bd) + EPS) * (B_HEAD_DIM ** -0.5)
        k2 = k2 * lax.rsqrt(_dot_exact_rhs(k2 * k2, ones_bd) + EPS)
        bcol = [beta[:, GDN_BETA_LANE + 2 * p + e:GDN_BETA_LANE + 2 * p + e + 1] for e in range(2)]
        gcol = [gc[:, GDN_DECAY_LANE + 2 * p + e:GDN_DECAY_LANE + 2 * p + e + 1] for e in range(2)]
        glast = [gcol[e][C - 1:C, :] for e in range(2)]
        beta2 = pair(bcol[0], bcol[1])
        gam2 = jnp.exp(pair(gcol[0], gcol[1]))
        kd2 = k2 * jnp.exp(pair(glast[0] - gcol[0], glast[1] - gcol[1]))
        gl2 = jnp.exp(jnp.where(first[0:1, :], glast[0], glast[1]))
        rhs = twice(jnp.concatenate([v2 * beta2, k2 * (beta2 * gam2)], axis=1))
        grow = jnp.concatenate([gct[GDN_DECAY_LANE + 2 * p + e:GDN_DECAY_LANE + 2 * p + e + 1, :]
                                for e in range(2)], axis=1)
        decay = jnp.exp(jnp.where(incl, jnp.concatenate(gcol, axis=0) - grow, -jnp.inf))
        kx = jnp.where(own_lanes, twice(k2), 0.0)
        qx = jnp.where(own_lanes, twice(q2), 0.0)
        a = jnp.where(strict, jnp.concatenate(bcol, axis=0) * _dot3(kx, kx, nt=True) * decay, 0.0)
        aqk = _dot_nt(qx.astype(BF16), kx.astype(BF16)) * decay
        st.append(dict(sl=sl, q2=q2, gam2=gam2, kd2=kd2, gl2=gl2, rhs=rhs, aqk=aqk, a=a))

    tinv = [eye - d["a"] for d in st]
    xk = [d["a"] for d in st]
    span = 2
    while span < C:
        xk = [_dot3(x, x) for x in xk]
        tinv = [t + _dot3(t, x) for t, x in zip(tinv, xk)]
        span *= 2
    sols = [_dot3(t, d["rhs"]) for t, d in zip(tinv, st)]

    for p, (d, sol) in enumerate(zip(st, sols)):
        u2 = halves(sol[:, :LANES])
        w2 = halves(sol[:, LANES:])
        s_old = s_ref[p]
        sb = s_old.astype(BF16)
        delta = u2 - _dot(w2.astype(BF16), sb)
        db = delta.astype(BF16)
        o2 = _dot((d["q2"] * d["gam2"]).astype(BF16), sb) + halves(_dot(d["aqk"].astype(BF16), twice(db)))
        upd = lax.dot_general(d["kd2"].astype(BF16), db, (((0,), (0,)), ((), ())),
                              preferred_element_type=F32)
        s_new = s_old * d["gl2"] + jnp.where(blockdiag, upd, 0.0)
        s_ref[p] = s_new
        sout_ref[0, p] = s_new
        ms = _dot_exact_rhs(o2 * o2, ones_bd) * (1.0 / B_HEAD_DIM)
        o_ref[:, d["sl"]] = (o2 * lax.rsqrt(ms + EPS) * ng_ref[...]
                             * _silu(z_ref[:, d["sl"]])).astype(o_ref.dtype)


def _gdn(xin, z, misc, tail0, s0, conv_w, a_log, dt_bias, norm_g, *, B, L):
    C = math.gcd(L, GDN_CHUNK)
    n = L // C
    M = B * L
    NP = B_HEADS // 2
    nrow = GDN_DECAY_LANE + B_HEADS
    misc_t = jnp.swapaxes(misc[:, :nrow].reshape(M // C, C, nrow), 1, 2)
    lane_vec = lambda v: jnp.zeros((1, LANES), F32).at[0, GDN_DECAY_LANE:nrow].set(v)
    col_vec = lambda v: jnp.zeros((nrow, 1), F32).at[GDN_DECAY_LANE:, 0].set(v)
    cw = jnp.pad(conv_w, ((0, SUBLANES - CONV_WIDTH), (0, 0)))
    ng = jnp.tile(norm_g.reshape(1, B_HEAD_DIM), (1, 2))
    full = lambda shape: pl.BlockSpec(shape, lambda b, c: (0,) * len(shape))
    return pl.pallas_call(
        functools.partial(_gdn_kernel, C=C),
        grid=(B, n),
        in_specs=[pl.BlockSpec((C, GDN_CONV_DIM), lambda b, c: (b * n + c, 0)),
                  pl.BlockSpec((C, GDN_QK), lambda b, c: (b * n + c, 0)),
                  pl.BlockSpec((C, LANES), lambda b, c: (b * n + c, 0)),
                  pl.BlockSpec((1, nrow, C), lambda b, c: (b * n + c, 0, 0)),
                  pl.BlockSpec((1, GDN_TAIL, GDN_CONV_DIM), lambda b, c: (b, 0, 0)),
                  pl.BlockSpec((1, NP, LANES, LANES), lambda b, c: (b, 0, 0, 0)),
                  full((SUBLANES, GDN_CONV_DIM)), full((1, LANES)), full((1, LANES)),
                  full((nrow, 1)), full((nrow, 1)), full((1, LANES))],
        out_specs=[pl.BlockSpec((C, GDN_QK), lambda b, c: (b * n + c, 0)),
                   pl.BlockSpec((1, GDN_TAIL, GDN_CONV_DIM), lambda b, c: (b, 0, 0)),
                   pl.BlockSpec((1, NP, LANES, LANES), lambda b, c: (b, 0, 0, 0))],
        out_shape=[jax.ShapeDtypeStruct((M, GDN_QK), BF16),
                   jax.ShapeDtypeStruct((B, GDN_TAIL, GDN_CONV_DIM), F32),
                   jax.ShapeDtypeStruct((B, NP, LANES, LANES), F32)],
        scratch_shapes=[pltpu.VMEM((C + GDN_TAIL, GDN_CONV_DIM), F32),
                        pltpu.VMEM((NP, LANES, LANES), F32)],
        compiler_params=_params("arbitrary", "arbitrary"), name="gdn",
    )(xin, z, misc, misc_t, tail0, s0, cw, lane_vec(a_log), lane_vec(dt_bias),
      col_vec(a_log), col_vec(dt_bias), ng)


def _state_to_blockdiag(s):
    B = s.shape[0]
    d = B_HEAD_DIM
    s = s.reshape(B, B_HEADS // 2, 2, d, d)
    z = jnp.zeros_like(s[:, :, 0])
    top = jnp.concatenate([s[:, :, 0], z], axis=-1)
    bot = jnp.concatenate([z, s[:, :, 1]], axis=-1)
    return jnp.concatenate([top, bot], axis=-2)


def _state_from_blockdiag(s):
    d = B_HEAD_DIM
    return jnp.stack([s[:, :, :d, :d], s[:, :, d:, d:]], axis=2).reshape(s.shape[0], B_HEADS, d, d)


PAGES_PER_STEP = 16
PAGES_PER_BLOCK = 4
SELECT_ROWS = 64


def _page_specs(block, layer, n):
    nd = len(block)

    def spec(p):
        return pl.BlockSpec(block, lambda b, j, pt, p=p: (pt[b, j * n + p], layer) + (0,) * (nd - 2))

    return [spec(p) for p in range(n)]


def _stack_heads(x, n_heads, width=LANES):
    return jnp.concatenate([x[:, h * width:(h + 1) * width] for h in range(n_heads)], axis=0)


def _pad_rows(x, rows):
    return jnp.concatenate([x, jnp.zeros((rows - x.shape[0], x.shape[1]), x.dtype)], axis=0)


def _dsa_s_score_kernel(pt_ref, iq_ref, misc_ref, *refs, n_pg, T):
    pages, o_ref = refs[:n_pg], refs[n_pg]
    q = _stack_heads(iq_ref[...], IDX_HEADS)[:, :IDX_DIM].astype(BF16)
    misc = misc_ref[...]
    w = jnp.concatenate([misc[:, h:h + 1] for h in range(IDX_HEADS)], axis=0)
    keys_t = jnp.concatenate([pages[p][0, 0].astype(BF16) for p in range(n_pg)], axis=1)
    r = jnp.maximum(_dot(q, keys_t), 0.0) * w
    sc = r[0:T]
    for h in range(1, IDX_HEADS):
        sc = sc + r[h * T:(h + 1) * T]
    o_ref[0] = sc


def _dsa_s_select_kernel(sp_ref, iq_ref, ikn_ref, misc_ref, bp_ref, bn_ref, *, G, T, topk):
    past = sp_ref.shape[2]
    R = G * T
    misc = misc_ref[...]
    iq = iq_ref[...]
    ikn = _pad_rows(ikn_ref[...], LANES).astype(BF16)
    sn = jnp.zeros((R, LANES), F32)
    for h in range(IDX_HEADS):
        s = _dot_nt(iq[:, h * LANES:(h + 1) * LANES].astype(BF16), ikn)
        sn = sn + misc[:, h:h + 1] * jnp.maximum(s, 0.0)
    row = lax.broadcasted_iota(I32, (R, LANES), 0)
    lane = lax.broadcasted_iota(I32, (R, LANES), 1)
    visible = (lane // T == row // T) & (lane % T <= row % T)
    key_new = jnp.where(visible, _order_key(sn), INT_MIN)
    key = jnp.concatenate([_order_key(sp_ref[...].reshape(R, past)), key_new], axis=1)
    idx = lax.broadcasted_iota(I32, key.shape, 1)
    sel = _topk_mask(key, idx, topk, 1, (past + LANES - 1).bit_length())
    bias = jnp.where(sel, 0.0, NEG_BIG)
    bp_ref[...] = bias[:, :past].reshape(G, T, past)
    bn_ref[...] = jnp.where(visible, bias[:, past:], NEG_BIG).reshape(G, T, LANES)


def _online_softmax_step(s, v, m_ref, l_ref, acc_ref, v_transposed=False):
    m_old = m_ref[...]
    m_new = jnp.maximum(m_old, jnp.max(s, axis=-1, keepdims=True))
    corr = jnp.exp(m_old - m_new)
    p = jnp.exp(s - m_new)
    l_ref[...] = l_ref[...] * corr + jnp.sum(p, axis=-1, keepdims=True)
    pv = _dot_nt(p.astype(BF16), v) if v_transposed else _dot(p.astype(BF16), v)
    acc_ref[...] = acc_ref[...] * corr + pv
    m_ref[...] = m_new


def _softmax_blocks_step(blocks, m_ref, l_ref, acc_ref, v_transposed=False):
    parts = []
    for s, v in blocks:
        m_b = jnp.max(s, axis=-1, keepdims=True)
        p = jnp.exp(s - m_b)
        pv = _dot_nt(p.astype(BF16), v) if v_transposed else _dot(p.astype(BF16), v)
        parts.append((m_b, jnp.sum(p, axis=-1, keepdims=True), pv))
    m_old = m_ref[...]
    m_new = m_old
    for m_b, _, _ in parts:
        m_new = jnp.maximum(m_new, m_b)
    corr = jnp.exp(m_old - m_new)
    l = l_ref[...] * corr
    acc = acc_ref[...] * corr
    for m_b, l_b, pv in parts:
        w = jnp.exp(m_b - m_new)
        l = l + l_b * w
        acc = acc + pv * w
    l_ref[...] = l
    acc_ref[...] = acc
    m_ref[...] = m_new


def _dsa_s_attend_kernel(pt_ref, aq_ref, akn_ref, avn_ref, bp_ref, bn_ref, *refs, n_pg, T):
    kpages, vpages = refs[:n_pg], refs[n_pg:2 * n_pg]
    o_ref, q_ref, m_ref, l_ref, acc_ref = refs[2 * n_pg:]
    j = pl.program_id(1)
    scale = A_HEAD_DIM ** -0.5

    @pl.when(j == 0)
    def _():
        q_ref[...] = _stack_heads(aq_ref[...], A_HEADS).astype(BF16)
        m_ref[...] = jnp.full(m_ref.shape, NEG_BIG, F32)
        l_ref[...] = jnp.zeros(l_ref.shape, F32)
        acc_ref[...] = jnp.zeros(acc_ref.shape, F32)
        kn = _pad_rows(akn_ref[...], LANES).astype(BF16)
        vn = _pad_rows(avn_ref[...], LANES).astype(BF16)
        s = _dot_nt(q_ref[...], kn) * scale + jnp.concatenate([bn_ref[0]] * A_HEADS, axis=0)
        _online_softmax_step(s, vn, m_ref, l_ref, acc_ref)

    q = q_ref[...]
    blocks = []
    for p0 in range(0, n_pg, PAGES_PER_BLOCK):
        group = range(p0, min(p0 + PAGES_PER_BLOCK, n_pg))
        k_t = jnp.concatenate([kpages[p][0, 0].astype(BF16) for p in group], axis=1)
        v_t = jnp.concatenate([vpages[p][0, 0].astype(BF16) for p in group], axis=1)
        ps = kpages[0].shape[3]
        bias = bp_ref[0, :, p0 * ps:(group[-1] + 1) * ps]
        blocks.append((_dot(q, k_t) * scale + jnp.concatenate([bias] * A_HEADS, axis=0), v_t))
    _softmax_blocks_step(blocks, m_ref, l_ref, acc_ref, v_transposed=True)

    @pl.when(j == pl.num_programs(1) - 1)
    def _():
        o = acc_ref[...] / l_ref[...]
        lane = lax.broadcasted_iota(I32, (T, LANES), 1)
        rep = A_HEADS // A_KV_HEADS
        for hp in range(A_HEADS // 2):
            a = o[(2 * hp) * T:(2 * hp + 1) * T]
            b = o[(2 * hp + 1) * T:(2 * hp + 2) * T]
            if (2 * hp) // rep == 0:
                b = pltpu.roll(b, A_HEAD_DIM, 1)
            else:
                a = pltpu.roll(a, A_HEAD_DIM, 1)
            o_ref[:, hp * LANES:(hp + 1) * LANES] = jnp.where(lane < A_HEAD_DIM, a, b).astype(o_ref.dtype)


def _dsa_sample(iq, ikn, misc, aq, akn, avn, pool_k, pool_v, pool_idx, page_table, layer, *, Bd, T):
    n_pages = page_table.shape[1]
    page = pool_idx.shape[2]
    past = n_pages * page
    n_pg = math.gcd(PAGES_PER_STEP, n_pages)
    npg = n_pages // n_pg
    topk = min(TOPK_MAX, (past + T) // 4)
    W = A_HEADS * LANES
    kv_dim = A_KV_HEADS * A_HEAD_DIM
    pk = jnp.transpose(pool_k, (0, 1, 3, 4, 2)).reshape(pool_k.shape[:2] + (kv_dim, page))
    pv = jnp.transpose(pool_v, (0, 1, 3, 4, 2)).reshape(pool_v.shape[:2] + (kv_dim, page))
    pi = jnp.swapaxes(pool_idx, 2, 3)
    row = lambda w: pl.BlockSpec((T, w), lambda b, j, pt: (b, 0))

    scores = pl.pallas_call(
        functools.partial(_dsa_s_score_kernel, n_pg=n_pg, T=T),
        grid_spec=pltpu.PrefetchScalarGridSpec(
            num_scalar_prefetch=1, grid=(Bd, npg),
            in_specs=[row(W), row(LANES)] + _page_specs((1, 1, IDX_DIM, page), layer, n_pg),
            out_specs=pl.BlockSpec((1, T, n_pg * page), lambda b, j, pt: (b, 0, j))),
        out_shape=jax.ShapeDtypeStruct((Bd, T, past), F32),
        compiler_params=_params("parallel", "arbitrary"), name="dsa_s_score",
    )(page_table, iq, misc, *([pi] * n_pg))

    G = math.gcd(Bd, max(1, SELECT_ROWS // T))
    assert G * T <= LANES
    bias_p, bias_n = pl.pallas_call(
        functools.partial(_dsa_s_select_kernel, G=G, T=T, topk=topk),
        grid=(Bd // G,),
        in_specs=[pl.BlockSpec((G, T, past), lambda b: (b, 0, 0)),
                  pl.BlockSpec((G * T, W), lambda b: (b, 0)),
                  pl.BlockSpec((G * T, LANES), lambda b: (b, 0)),
                  pl.BlockSpec((G * T, LANES), lambda b: (b, 0))],
        out_specs=[pl.BlockSpec((G, T, past), lambda b: (b, 0, 0)),
                   pl.BlockSpec((G, T, LANES), lambda b: (b, 0, 0))],
        out_shape=[jax.ShapeDtypeStruct((Bd, T, past), F32),
                   jax.ShapeDtypeStruct((Bd, T, LANES), F32)],
        compiler_params=_params("parallel"), name="dsa_s_select",
    )(scores, iq, ikn, misc)

    group_rows = pl.BlockSpec((G * T, LANES), lambda b, j, pt: (b // G, 0))
    return pl.pallas_call(
        functools.partial(_dsa_s_attend_kernel, n_pg=n_pg, T=T),
        grid_spec=pltpu.PrefetchScalarGridSpec(
            num_scalar_prefetch=1, grid=(Bd, npg),
            in_specs=[row(W), group_rows, group_rows,
                      pl.BlockSpec((1, T, n_pg * page), lambda b, j, pt: (b, 0, j)),
                      pl.BlockSpec((1, T, LANES), lambda b, j, pt: (b, 0, 0))]
            + _page_specs((1, 1, kv_dim, page), layer, n_pg) * 2,
            out_specs=pl.BlockSpec((T, A_HEADS * A_HEAD_DIM), lambda b, j, pt: (b, 0)),
            scratch_shapes=[pltpu.VMEM((A_HEADS * T, LANES), BF16),
                            pltpu.VMEM((A_HEADS * T, 1), F32),
                            pltpu.VMEM((A_HEADS * T, 1), F32),
                            pltpu.VMEM((A_HEADS * T, LANES), F32)]),
        out_shape=jax.ShapeDtypeStruct((Bd * T, A_HEADS * A_HEAD_DIM), BF16),
        compiler_params=_params("parallel", "arbitrary"), name="dsa_s_attend",
    )(page_table, aq, akn, avn, bias_p, bias_n, *([pk] * n_pg), *([pv] * n_pg))


def _head_matmul_kernel(x_ref, w_ref, o_ref):
    o_ref[0] = _dot(x_ref[...].astype(BF16), w_ref[0])


def _mla_s_attend_kernel(pt_ref, ql_ref, q_ref, latn_ref, krn_ref, *refs, n_pg, T):
    lpages, rpages = refs[:n_pg], refs[n_pg:2 * n_pg]
    o_ref, qlat_ref, qr_ref, m_ref, l_ref, acc_ref = refs[2 * n_pg:]
    j = pl.program_id(1)
    R = C_HEADS * T
    lo = C_NOPE

    @pl.when(j == 0)
    def _():
        qlat_ref[...] = ql_ref[...].reshape(R, C_KV_LORA).astype(BF16)
        q = q_ref[...]
        qr_ref[...] = jnp.concatenate(
            [q[:, h * LANES + lo:h * LANES + lo + C_ROPE] for h in range(C_HEADS)], axis=0).astype(BF16)
        m_ref[...] = jnp.full(m_ref.shape, NEG_BIG, F32)
        l_ref[...] = jnp.zeros(l_ref.shape, F32)
        acc_ref[...] = jnp.zeros(acc_ref.shape, F32)
        latn = _pad_rows(latn_ref[...], LANES).astype(BF16)
        krn = _pad_rows(krn_ref[...][:, lo:lo + C_ROPE], LANES).astype(BF16)
        s = (_dot_nt(qlat_ref[...], latn) + _dot_nt(qr_ref[...], krn)) * MLA_SCALE
        t_row = lax.broadcasted_iota(I32, (R, LANES), 0) % T
        n_key = lax.broadcasted_iota(I32, (R, LANES), 1)
        s = jnp.where(n_key <= t_row, s, NEG_BIG)
        _online_softmax_step(s, latn, m_ref, l_ref, acc_ref)

    ql = qlat_ref[...]
    qr = qr_ref[...]
    blocks = []
    for p0 in range(0, n_pg, PAGES_PER_BLOCK):
        group = range(p0, min(p0 + PAGES_PER_BLOCK, n_pg))
        lat = jnp.concatenate([lpages[p][0, 0].astype(BF16) for p in group], axis=0)
        kr_t = jnp.concatenate([rpages[p][0, 0].astype(BF16) for p in group], axis=1)
        blocks.append(((_dot_nt(ql, lat) + _dot(qr, kr_t)) * MLA_SCALE, lat))
    _softmax_blocks_step(blocks, m_ref, l_ref, acc_ref)

    @pl.when(j == pl.num_programs(1) - 1)
    def _():
        o_ref[...] = (acc_ref[...] / l_ref[...]).reshape(C_HEADS, T, C_KV_LORA)


def _pair_matmul_kernel(a_ref, b_ref, wa_ref, wb_ref, o_ref):
    o_ref[...] = (_dot(a_ref[0].astype(BF16), wa_ref[0])
                  + _dot(b_ref[0].astype(BF16), wb_ref[0])).astype(o_ref.dtype)


def _mla_sample(q, latn, krn, w_ukv, pool_lat, pool_kr, page_table, layer, *, Bd, T):
    Ms = Bd * T
    n_pages = page_table.shape[1]
    page = pool_lat.shape[2]
    n_pg = math.gcd(PAGES_PER_STEP, n_pages)
    npg = n_pages // n_pg
    w_uk_t = jnp.pad(jnp.transpose(w_ukv[..., :C_NOPE], (1, 2, 0)),
                     ((0, 0), (0, LANES - C_NOPE), (0, 0))).astype(BF16)
    wv = jnp.transpose(w_ukv[..., C_NOPE:], (1, 0, 2))
    even = (jnp.arange(C_HEADS) % 2 == 0)[:, None, None]
    z = jnp.zeros_like(wv)
    w_uv = jnp.concatenate([jnp.where(even, wv, z), jnp.where(even, z, wv)], axis=-1).astype(BF16)

    q_lat = pl.pallas_call(
        _head_matmul_kernel, grid=(C_HEADS,),
        in_specs=[pl.BlockSpec((Ms, LANES), lambda h: (0, h)),
                  pl.BlockSpec((1, LANES, C_KV_LORA), lambda h: (h, 0, 0))],
        out_specs=pl.BlockSpec((1, Ms, C_KV_LORA), lambda h: (h, 0, 0)),
        out_shape=jax.ShapeDtypeStruct((C_HEADS, Ms, C_KV_LORA), F32),
        compiler_params=_params("parallel"), name="mla_s_qlat",
    )(q, w_uk_t)

    R = C_HEADS * T
    row = lambda w: pl.BlockSpec((T, w), lambda b, j, pt: (b, 0))
    o_lat = pl.pallas_call(
        functools.partial(_mla_s_attend_kernel, n_pg=n_pg, T=T),
        grid_spec=pltpu.PrefetchScalarGridSpec(
            num_scalar_prefetch=1, grid=(Bd, npg),
            in_specs=[pl.BlockSpec((C_HEADS, T, C_KV_LORA), lambda b, j, pt: (0, b, 0)),
                      row(C_HEADS * LANES), row(C_KV_LORA), row(LANES)]
            + _page_specs((1, 1, page, C_KV_LORA), layer, n_pg)
            + _page_specs((1, 1, C_ROPE, page), layer, n_pg),
            out_specs=pl.BlockSpec((C_HEADS, T, C_KV_LORA), lambda b, j, pt: (0, b, 0)),
            scratch_shapes=[pltpu.VMEM((R, C_KV_LORA), BF16), pltpu.VMEM((R, C_ROPE), BF16),
                            pltpu.VMEM((R, 1), F32), pltpu.VMEM((R, 1), F32),
                            pltpu.VMEM((R, C_KV_LORA), F32)]),
        out_shape=jax.ShapeDtypeStruct((C_HEADS, Ms, C_KV_LORA), F32),
        compiler_params=_params("parallel", "arbitrary"), name="mla_s_attend",
    )(page_table, q_lat, q, latn, krn, *([pool_lat] * n_pg), *([jnp.swapaxes(pool_kr, 2, 3)] * n_pg))

    return pl.pallas_call(
        _pair_matmul_kernel, grid=(C_HEADS // 2,),
        in_specs=[pl.BlockSpec((1, Ms, C_KV_LORA), lambda h: (2 * h, 0, 0)),
                  pl.BlockSpec((1, Ms, C_KV_LORA), lambda h: (2 * h + 1, 0, 0)),
                  pl.BlockSpec((1, C_KV_LORA, LANES), lambda h: (2 * h, 0, 0)),
                  pl.BlockSpec((1, C_KV_LORA, LANES), lambda h: (2 * h + 1, 0, 0))],
        out_specs=pl.BlockSpec((Ms, LANES), lambda h: (0, h)),
        out_shape=jax.ShapeDtypeStruct((Ms, C_HEADS * C_V), BF16),
        compiler_params=_params("parallel"), name="mla_s_out",
    )(o_lat, o_lat, w_uv, w_uv)


def _mla_weights(w_down, w_uq, w_ukv):
    D = w_down.shape[0]
    kr_cols = w_down[:, C_Q_LORA + C_KV_LORA:]
    wd = jnp.concatenate([w_down[:, :C_Q_LORA + C_KV_LORA], jnp.zeros((D, C_NOPE), F32), kr_cols,
                          jnp.zeros((D, LANES - C_NOPE - C_ROPE), F32)], axis=1)
    wq = jnp.pad(w_uq, ((0, 0), (0, 0), (0, LANES - C_NOPE - C_ROPE))).reshape(C_Q_LORA, C_HEADS * LANES)
    wkv = w_ukv.reshape(C_KV_LORA, C_HEADS * LANES)
    return wd.astype(BF16), wq.astype(BF16), wkv.astype(BF16)


def _mla_project(x, g_mix, wd, wq, wkv, g_q, g_kv, rope, tm, qdtype):
    cq, ckv, kr = _proj(x, wd, ((C_Q_LORA, False, F32), (C_KV_LORA, False, F32), (LANES, True, F32)),
                        gain=g_mix, rope=rope, tm=tm, name="mla_down")
    (q,) = _proj(cq, wq, ((C_HEADS * LANES, True, qdtype),), gain=g_q, rope=rope, tm=tm, name="mla_uq")
    kv, lat = _proj(ckv, wkv, ((C_HEADS * LANES, False, BF16),), gain=g_kv, emit_norm=True, tm=tm,
                    name="mla_ukv")
    return q, kv, lat, kr


def _even_weights(w_in):
    D = w_in.shape[0]
    sizes = (A_HEADS * A_HEAD_DIM, A_KV_HEADS * A_HEAD_DIM, A_KV_HEADS * A_HEAD_DIM,
             IDX_HEADS * IDX_DIM, IDX_DIM, IDX_HEADS, GDN_QK, GDN_QK, GDN_QK, GDN_QK, B_HEADS, B_HEADS)
    parts, c = [], 0
    for s in sizes:
        parts.append(w_in[:, c:c + s])
        c += s
    aq, ak, av, iq, ik, iw, bq, bk, bv, bz, bb, ba = parts
    zero = jnp.zeros((D, A_HEADS, A_HEAD_DIM), F32)
    aq = aq.reshape(D, A_HEADS, A_HEAD_DIM)
    in_g0 = (jnp.arange(A_HEADS) < A_HEADS // A_KV_HEADS)[None, :, None]
    aq128 = jnp.concatenate([jnp.where(in_g0, aq, zero), jnp.where(in_g0, zero, aq)], axis=-1)
    iq128 = jnp.concatenate([iq.reshape(D, IDX_HEADS, IDX_DIM), zero], axis=-1)
    ik128 = jnp.pad(ik, ((0, 0), (0, LANES - IDX_DIM)))
    misc = jnp.pad(jnp.concatenate([iw, bb, ba], axis=1), ((0, 0), (0, LANES - IDX_HEADS - 2 * B_HEADS)))
    w = jnp.concatenate([aq128.reshape(D, -1), ak, iq128.reshape(D, -1), ik128, av, misc, bq, bk, bv, bz],
                        axis=1)
    misc_col = A_HEADS * LANES + LANES + IDX_HEADS * LANES + LANES + LANES
    scale = jnp.ones((w.shape[1],), F32).at[misc_col:misc_col + IDX_HEADS].set(IDX_W_SCALE)
    return w.astype(BF16), scale


def _even_groups(qdtype):
    return ((A_HEADS * LANES, True, qdtype), (LANES, True, F32), (IDX_HEADS * LANES, True, qdtype),
            (LANES, True, F32), (LANES, False, F32), (LANES, False, F32),
            (GDN_CONV_DIM, False, F32), (GDN_QK, False, F32))


def _tile_rows(t, rows):
    return jnp.tile(t, (rows // t.shape[0], 1))


def kernel(x_prompt, x_sample, cache_a_k, cache_a_v, cache_a_idx, state_b_ssm, state_b_conv,
           cache_c_latent, cache_c_krope, cache_mem_k, cache_mem_v, page_table, mem_prompt,
           g_mix, g_cross, g_mem, g_mlp, g_final,
           w_in_even, gdn_conv_w, gdn_a_log, gdn_dt_bias, gdn_norm_g, w_out_even,
           w_down_odd, g_q_lora, g_kv_lora, w_uq, w_ukv, w_out_odd,
           w_xq, w_xk, w_xv, w_xo, w_ff1, w_ff2):
    Bp, S, D = x_prompt.shape
    Bd, T, _ = x_sample.shape
    depth = g_mix.shape[0]
    Mp, Ms = Bp * S, Bd * T
    past_len = page_table.shape[1] * cache_a_k.shape[2]
    pos_p = jnp.arange(S, dtype=jnp.int32)
    pos_s = past_len + jnp.arange(T, dtype=jnp.int32)
    tm_p = min(256, S)
    tm_s = min(256, Ms)

    def rope_pair(dim, lo, hi):
        cp, sap, sbp, half = _rope_tables(pos_p, dim, lo, hi)
        cs, sas, sbs, _ = _rope_tables(pos_s, dim, lo, hi)
        return ((cp, sap, sbp, half, S // tm_p),
                (_tile_rows(cs, tm_s), _tile_rows(sas, tm_s), _tile_rows(sbs, tm_s), half, 1))

    rope_even_p, rope_even_s = rope_pair(A_HEAD_DIM, 0, LANES)
    rope_odd_p, rope_odd_s = rope_pair(C_ROPE, C_NOPE, C_NOPE + C_ROPE)

    xp = x_prompt.reshape(Mp, D)
    xs = x_sample.reshape(Ms, D)
    mem = mem_prompt.reshape(Bp * N_MEM, D)
    res = ((D, False, F32),)
    pa_k, pa_v, pa_i, pb_s, pb_c, pc_l, pc_r, pm_k, pm_v = [], [], [], [], [], [], [], [], []
    sa_k, sa_v, sa_i, sb_s, sb_c, sc_l, sc_r = [], [], [], [], [], [], []
    for li in range(depth):
        if li % 2 == 0:
            e = li // 2
            w_even, cscale = _even_weights(w_in_even[e])
            w_out = w_out_even[e].astype(BF16)
            gdn_w = (gdn_conv_w[e], gdn_a_log[e], gdn_dt_bias[e], gdn_norm_g[e])
            aq, ak, iq, ik, av, misc, conv_in, bz = _proj(
                xp, w_even, _even_groups(BF16), gain=g_mix[li], rope=rope_even_p, colscale=cscale,
                tm=tm_p, name="even_in")
            a_out = _dsa_prompt(iq, ik, misc[:, :IDX_HEADS].T, aq, ak,
                                jnp.swapaxes(av.reshape(Bp, S, LANES), 1, 2), B=Bp, S=S)
            b_out, tail, sbd = _gdn(
                conv_in, bz, misc, jnp.zeros((Bp, GDN_TAIL, GDN_CONV_DIM), F32),
                jnp.zeros((Bp, B_HEADS // 2, LANES, LANES), F32), *gdn_w, B=Bp, L=S)
            (xp,) = _proj(jnp.concatenate([a_out, b_out], axis=1), w_out, res, residual=xp, tm=tm_p,
                          name="even_out")
            pa_k.append(ak.reshape(Bp, S, A_KV_HEADS, A_HEAD_DIM))
            pa_v.append(av.reshape(Bp, S, A_KV_HEADS, A_HEAD_DIM))
            pa_i.append(ik[:, :IDX_DIM].reshape(Bp, S, IDX_DIM))
            pb_s.append(_state_from_blockdiag(sbd))
            pb_c.append(tail[:, GDN_TAIL - (CONV_WIDTH - 1):])
            aq, ak, iq, ik, av, misc, conv_in, bz = _proj(
                xs, w_even, _even_groups(F32), gain=g_mix[li], rope=rope_even_s, colscale=cscale,
                tm=tm_s, name="even_in_s")
            a_out = _dsa_sample(iq, ik, misc, aq, ak, av, cache_a_k, cache_a_v, cache_a_idx,
                                page_table, e, Bd=Bd, T=T)
            tail0 = jnp.pad(state_b_conv[e], ((0, 0), (GDN_TAIL - (CONV_WIDTH - 1), 0), (0, 0)))
            b_out, tail, sbd = _gdn(conv_in, bz, misc, tail0, _state_to_blockdiag(state_b_ssm[e]),
                                    *gdn_w, B=Bd, L=T)
            (xs,) = _proj(jnp.concatenate([a_out, b_out], axis=1), w_out, res, residual=xs, tm=tm_s,
                          name="even_out_s")
            sa_k.append(ak.reshape(Bd, T, A_KV_HEADS, A_HEAD_DIM))
            sa_v.append(av.reshape(Bd, T, A_KV_HEADS, A_HEAD_DIM))
            sa_i.append(ik[:, :IDX_DIM].reshape(Bd, T, IDX_DIM))
            sb_s.append(_state_from_blockdiag(sbd).astype(state_b_ssm.dtype))
            sb_c.append(tail[:, GDN_TAIL - (CONV_WIDTH - 1):])
        else:
            o = li // 2
            wd, wq, wkv = _mla_weights(w_down_odd[o], w_uq[o], w_ukv[o])
            w_out = w_out_odd[o].astype(BF16)
            q, kv, lat, kr = _mla_project(xp, g_mix[li], wd, wq, wkv, g_q_lora[o], g_kv_lora[o],
                                          rope_odd_p, tm_p, BF16)
            (xp,) = _proj(_mla_prompt(q, kv, kr, B=Bp, S=S), w_out, res, residual=xp, tm=tm_p,
                          name="odd_out")
            pc_l.append(lat.reshape(Bp, S, C_KV_LORA))
            pc_r.append(kr[:, C_NOPE:C_NOPE + C_ROPE].reshape(Bp, S, C_ROPE))
            q, kv, lat, kr = _mla_project(xs, g_mix[li], wd, wq, wkv, g_q_lora[o], g_kv_lora[o],
                                          rope_odd_s, tm_s, F32)
            a_out = _mla_sample(q, lat, kr, w_ukv[o], cache_c_latent, cache_c_krope, page_table, o,
                                Bd=Bd, T=T)
            (xs,) = _proj(a_out, w_out, res, residual=xs, tm=tm_s, name="odd_out_s")
            sc_l.append(lat.reshape(Bd, T, C_KV_LORA))
            sc_r.append(kr[:, C_NOPE:C_NOPE + C_ROPE].reshape(Bd, T, C_ROPE))
        w_kv = jnp.concatenate([w_xk[li], w_xv[li]], axis=1).astype(BF16)
        mk, mv = _proj(mem, w_kv, ((X_WIDTH, False, F32), (X_WIDTH, False, F32)), gain=g_mem[li],
                       tm=min(256, mem.shape[0]), name="mem_kv")
        pm_k.append(mk.reshape(Bp, N_MEM, X_HEADS, X_HEAD_DIM))
        pm_v.append(mv.reshape(Bp, N_MEM, X_HEADS, X_HEAD_DIM))
        wxq, wxo = w_xq[li].astype(BF16), w_xo[li].astype(BF16)
        xp = _xattn(xp, g_cross[li], wxq, mk.reshape(Bp, N_MEM, X_WIDTH), mv.reshape(Bp, N_MEM, X_WIDTH),
                    wxo, rows_per_batch=S, tm=tm_p)
        xs = _xattn(xs, g_cross[li], wxq, cache_mem_k[li].reshape(Bd, N_MEM, X_WIDTH),
                    cache_mem_v[li].reshape(Bd, N_MEM, X_WIDTH), wxo, rows_per_batch=T, tm=T)
        w1, w2 = w_ff1[li].astype(BF16), w_ff2[li].astype(BF16)
        last = li == depth - 1
        xp = _mlp(xp, g_mlp[li], w1, w2, g_final, final_norm=last)
        xs = _mlp(xs, g_mlp[li], w1, w2, g_final, final_norm=last)
    return (xp.reshape(Bp, S, D), xs.reshape(Bd, T, D),
            jnp.stack(pa_k, axis=1), jnp.stack(pa_v, axis=1), jnp.stack(pa_i, axis=1),
            jnp.stack(pb_s, axis=0), jnp.stack(pb_c, axis=0),
            jnp.stack(pc_l, axis=1), jnp.stack(pc_r, axis=1),
            jnp.stack(pm_k, axis=0), jnp.stack(pm_v, axis=0),
            jnp.stack(sa_k, axis=1), jnp.stack(sa_v, axis=1), jnp.stack(sa_i, axis=1),
            jnp.stack(sb_s, axis=0), jnp.stack(sb_c, axis=0),
            jnp.stack(sc_l, axis=1), jnp.stack(sc_r, axis=1))
```

```python
import functools
import math

import jax
import jax.numpy as jnp
from jax import lax
from jax.experimental import pallas as pl
from jax.experimental.pallas import tpu as pltpu

F32 = jnp.float32
BF16 = jnp.bfloat16
I32 = jnp.int32

LANES = 128
SUBLANES = 8
VMEM_LIMIT = 56 * 1024 * 1024

EPS = 1e-6
ROPE_THETA = 10000.0
N_MEM = 256
TOPK_MAX = 256

A_HEADS = 8
A_KV_HEADS = 2
A_HEAD_DIM = 64
IDX_HEADS = 8
IDX_DIM = 64
IDX_W_SCALE = (IDX_HEADS * IDX_DIM) ** -0.5

B_HEADS = 8
B_HEAD_DIM = 64
CONV_WIDTH = 4
GDN_CHUNK = 64
GDN_QK = B_HEADS * B_HEAD_DIM
GDN_CONV_DIM = 3 * GDN_QK

C_HEADS = 16
C_NOPE = 64
C_ROPE = 32
C_V = 64
C_Q_LORA = 384
C_KV_LORA = 256
MLA_SCALE = (C_NOPE + C_ROPE) ** -0.5

X_HEADS = 4
X_HEAD_DIM = 128
X_WIDTH = X_HEADS * X_HEAD_DIM
X_SCALE = X_HEAD_DIM ** -0.5

INT_MIN = -2 ** 31
NEG_BIG = -1e30


def _params(*sem):
    return pltpu.CompilerParams(dimension_semantics=sem, vmem_limit_bytes=VMEM_LIMIT)


def _rms(x, g):
    return x * lax.rsqrt(jnp.mean(x * x, axis=-1, keepdims=True) + EPS) * g


def _dot(a, b):
    return jnp.dot(a, b, preferred_element_type=F32)


def _dot_nt(a, b):
    return lax.dot_general(a, b, (((1,), (1,)), ((), ())), preferred_element_type=F32)


def _split3(a):
    a1 = a.astype(BF16)
    r = a - a1.astype(F32)
    a2 = r.astype(BF16)
    a3 = (r - a2.astype(F32)).astype(BF16)
    return a1, a2, a3


def _dot_hi(a, b, nt=False):
    f = _dot_nt if nt else _dot
    a1, a2, a3 = _split3(a)
    b1, b2, b3 = _split3(b)
    small = f(a1, b3) + f(a3, b1) + f(a2, b2)
    mid = f(a1, b2) + f(a2, b1)
    return f(a1, b1) + (mid + small)


def _dot_exact_rhs(a, b_bf16):
    a1, a2, a3 = _split3(a)
    return _dot(a1, b_bf16) + (_dot(a2, b_bf16) + _dot(a3, b_bf16))


def _proj_kernel(*refs, groups, has_gain, has_rope, has_scale, has_res, emit_norm, shift):
    it = iter(refs)
    x_ref = next(it)
    g_ref = next(it) if has_gain else None
    w_ref = next(it)
    if has_rope:
        c_ref, sa_ref, sb_ref = next(it), next(it), next(it)
    s_ref = next(it) if has_scale else None
    r_ref = next(it) if has_res else None
    outs = list(it)
    x = x_ref[...].astype(F32)
    if has_gain:
        x = _rms(x, g_ref[...])
    if emit_norm:
        outs[-1][...] = x
    xb = x.astype(BF16)
    col = 0
    plain = iter(outs[:sum(1 for g in groups if g[2] is not None)])
    flipped = iter(outs[sum(1 for g in groups if g[2] is not None):])
    for width, rope, dtype, t_rows in groups:
        y = _dot(xb, w_ref[:, col:col + width])
        if has_scale:
            y = y * s_ref[:, col:col + width]
        if rope:
            n = width // LANES
            c = jnp.concatenate([c_ref[...]] * n, axis=1)
            sa = jnp.concatenate([sa_ref[...]] * n, axis=1)
            sb = jnp.concatenate([sb_ref[...]] * n, axis=1)
            y = y * c + pltpu.roll(y, shift, 1) * sa + pltpu.roll(y, width - shift, 1) * sb
        if has_res:
            y = y + r_ref[...]
        if dtype is not None:
            next(plain)[...] = y.astype(dtype)
        if t_rows:
            next(flipped)[0] = y.T
        col += width


def _proj(x, w, groups, *, gain=None, rope=None, colscale=None, residual=None,
          emit_norm=False, tm=256, name="proj"):
    M, K = x.shape
    N = w.shape[1]
    groups = tuple(tuple(g) + (0,) * (4 - len(g)) for g in groups)
    assert sum(g[0] for g in groups) == N and M % tm == 0
    args, specs = [x], [pl.BlockSpec((tm, K), lambda i: (i, 0))]
    if gain is not None:
        args.append(gain.reshape(1, K).astype(F32))
        specs.append(pl.BlockSpec((1, K), lambda i: (0, 0)))
    args.append(w)
    specs.append(pl.BlockSpec((K, N), lambda i: (0, 0)))
    shift = 0
    if rope is not None:
        c, sa, sb, shift, nblk = rope
        for t in (c, sa, sb):
            args.append(t)
            specs.append(pl.BlockSpec((tm, LANES), lambda i, nblk=nblk: (i % nblk, 0)))
    if colscale is not None:
        args.append(colscale.reshape(1, N).astype(F32))
        specs.append(pl.BlockSpec((1, N), lambda i: (0, 0)))
    if residual is not None:
        assert len(groups) == 1
        args.append(residual)
        specs.append(pl.BlockSpec((tm, N), lambda i: (i, 0)))
    out_shape = [jax.ShapeDtypeStruct((M, g[0]), g[2]) for g in groups if g[2] is not None]
    out_specs = [pl.BlockSpec((tm, g[0]), lambda i: (i, 0)) for g in groups if g[2] is not None]
    for width, _, _, t_rows in groups:
        if t_rows:
            per = t_rows // tm
            assert t_rows % tm == 0
            out_shape.append(jax.ShapeDtypeStruct((M // t_rows, width, t_rows), F32))
            out_specs.append(pl.BlockSpec((1, width, tm), lambda i, per=per: (i // per, 0, i % per)))
    if emit_norm:
        out_shape.append(jax.ShapeDtypeStruct((M, K), F32))
        out_specs.append(pl.BlockSpec((tm, K), lambda i: (i, 0)))
    kern = functools.partial(
        _proj_kernel, groups=groups, has_gain=gain is not None,
        has_rope=rope is not None, has_scale=colscale is not None,
        has_res=residual is not None, emit_norm=emit_norm, shift=shift)
    return pl.pallas_call(
        kern, grid=(M // tm,), in_specs=specs, out_specs=out_specs,
        out_shape=out_shape, compiler_params=_params("parallel"), name=name)(*args)


def _mlp_kernel(x_ref, g_ref, w1_ref, w2_ref, gf_ref, o_ref, hn_ref, acc_ref, *, final_norm):
    j = pl.program_id(1)

    @pl.when(j == 0)
    def _():
        hn_ref[...] = _rms(x_ref[...], g_ref[...]).astype(BF16)
        acc_ref[...] = jnp.zeros_like(acc_ref)

    a = _dot(hn_ref[...], w1_ref[...])
    a = jnp.square(jnp.maximum(a, 0.0)).astype(BF16)
    acc_ref[...] += _dot(a, w2_ref[...])

    @pl.when(j == pl.num_programs(1) - 1)
    def _():
        y = x_ref[...] + acc_ref[...]
        if final_norm:
            y = _rms(y, gf_ref[...])
        o_ref[...] = y


def _mlp(x, g, w1, w2, g_final, *, final_norm, tm=512, tf=1024):
    M, D = x.shape
    F = w1.shape[1]
    tm = min(tm, M)
    assert M % tm == 0 and F % tf == 0
    return pl.pallas_call(
        functools.partial(_mlp_kernel, final_norm=final_norm),
        grid=(M // tm, F // tf),
        in_specs=[pl.BlockSpec((tm, D), lambda i, j: (i, 0)),
                  pl.BlockSpec((1, D), lambda i, j: (0, 0)),
                  pl.BlockSpec((D, tf), lambda i, j: (0, j)),
                  pl.BlockSpec((tf, D), lambda i, j: (j, 0)),
                  pl.BlockSpec((1, D), lambda i, j: (0, 0))],
        out_specs=pl.BlockSpec((tm, D), lambda i, j: (i, 0)),
        out_shape=jax.ShapeDtypeStruct((M, D), F32),
        scratch_shapes=[pltpu.VMEM((tm, D), BF16), pltpu.VMEM((tm, D), F32)],
        compiler_params=_params("parallel", "arbitrary"), name="mlp",
    )(x, g.reshape(1, D), w1, w2, g_final.reshape(1, D))


def _xattn_kernel(x_ref, g_ref, wq_ref, mk_ref, mv_ref, wo_ref, o_ref):
    x = x_ref[...]
    q = _dot(_rms(x, g_ref[...]).astype(BF16), wq_ref[...])
    mk = mk_ref[0].astype(BF16)
    mv = mv_ref[0].astype(BF16)
    heads = []
    for h in range(X_HEADS):
        sl = slice(h * X_HEAD_DIM, (h + 1) * X_HEAD_DIM)
        s = _dot_nt(q[:, sl].astype(BF16), mk[:, sl]) * X_SCALE
        s = s - jnp.max(s, axis=-1, keepdims=True)
        p = jnp.exp(s)
        p = p / jnp.sum(p, axis=-1, keepdims=True)
        heads.append(_dot(p.astype(BF16), mv[:, sl]))
    o = jnp.concatenate(heads, axis=1).astype(BF16)
    o_ref[...] = x + _dot(o, wo_ref[...])


def _xattn(x, g, wq, mk, mv, wo, *, rows_per_batch, tm):
    M, D = x.shape
    assert rows_per_batch % tm == 0
    per = rows_per_batch // tm
    nm = mk.shape[1]
    return pl.pallas_call(
        _xattn_kernel, grid=(M // tm,),
        in_specs=[pl.BlockSpec((tm, D), lambda i: (i, 0)),
                  pl.BlockSpec((1, D), lambda i: (0, 0)),
                  pl.BlockSpec((D, X_WIDTH), lambda i: (0, 0)),
                  pl.BlockSpec((1, nm, X_WIDTH), lambda i: (i // per, 0, 0)),
                  pl.BlockSpec((1, nm, X_WIDTH), lambda i: (i // per, 0, 0)),
                  pl.BlockSpec((X_WIDTH, D), lambda i: (0, 0))],
        out_specs=pl.BlockSpec((tm, D), lambda i: (i, 0)),
        out_shape=jax.ShapeDtypeStruct((M, D), F32),
        compiler_params=_params("parallel"), name="xattn",
    )(x, g.reshape(1, D), wq, mk, mv, wo)


def _xattn_rows_kernel(x_ref, g_ref, wq_ref, mk_ref, mv_ref, wo_ref, o_ref, *, G, T):
    x = x_ref[...]
    q = _dot(_rms(x, g_ref[...]).astype(BF16), wq_ref[...])
    rows = mk_ref.shape[1]
    own = (lax.broadcasted_iota(I32, (X_HEADS * T, rows), 1) % X_HEADS
           == lax.broadcasted_iota(I32, (X_HEADS * T, rows), 0) // T)
    outs = []
    for g in range(G):
        qs = _stack_heads(q[g * T:(g + 1) * T], X_HEADS, X_HEAD_DIM).astype(BF16)
        s = jnp.where(own, _dot_nt(qs, mk_ref[g].astype(BF16)) * X_SCALE, -jnp.inf)
        p = jnp.exp(s - jnp.max(s, axis=-1, keepdims=True))
        p = p / jnp.sum(p, axis=-1, keepdims=True)
        o = _dot(p.astype(BF16), mv_ref[g].astype(BF16))
        outs.append(jnp.concatenate([o[h * T:(h + 1) * T] for h in range(X_HEADS)], axis=1))
    o_all = jnp.concatenate(outs, axis=0).astype(BF16)
    o_ref[...] = x + _dot(o_all, wo_ref[...])


def _xattn_rows(x, g, wq, mk, mv, wo, *, T, G):
    M, D = x.shape
    B = M // T
    assert B % G == 0
    rows = mk.shape[1] * X_HEADS
    mk = mk.reshape(B, rows, X_HEAD_DIM)
    mv = mv.reshape(B, rows, X_HEAD_DIM)
    return pl.pallas_call(
        functools.partial(_xattn_rows_kernel, G=G, T=T), grid=(B // G,),
        in_specs=[pl.BlockSpec((G * T, D), lambda i: (i, 0)),
                  pl.BlockSpec((1, D), lambda i: (0, 0)),
                  pl.BlockSpec((D, X_WIDTH), lambda i: (0, 0)),
                  pl.BlockSpec((G, rows, X_HEAD_DIM), lambda i: (i, 0, 0)),
                  pl.BlockSpec((G, rows, X_HEAD_DIM), lambda i: (i, 0, 0)),
                  pl.BlockSpec((X_WIDTH, D), lambda i: (0, 0))],
        out_specs=pl.BlockSpec((G * T, D), lambda i: (i, 0)),
        out_shape=jax.ShapeDtypeStruct((M, D), F32),
        compiler_params=_params("parallel"), name="xattn_rows",
    )(x, g.reshape(1, D), wq, mk, mv, wo)


def _rope_tables(pos, dim, lane_lo, lane_hi):
    half = dim // 2
    lane = jnp.arange(LANES)
    j = (lane - lane_lo) % dim
    inside = (lane >= lane_lo) & (lane < lane_hi)
    inv = jnp.exp(-math.log(ROPE_THETA) * (j % half).astype(F32) / half)
    ang = pos.astype(F32)[:, None] * inv[None, :]
    cos, sin = jnp.cos(ang), jnp.sin(ang)
    c = jnp.where(inside[None, :], cos, 1.0)
    sa = jnp.where((inside & (j >= half))[None, :], sin, 0.0)
    sb = jnp.where((inside & (j < half))[None, :], -sin, 0.0)
    return c, sa, sb, half


def _mla_prompt_block(q_ref, kv_ref, kr_ref, o_ref, n, tq):
    lo = n * tq
    row = lax.broadcasted_iota(I32, (tq, tq), 0)
    col = lax.broadcasted_iota(I32, (tq, tq), 1)

    def keys(a, b, h):
        kv = kv_ref[a:b, h * LANES:(h + 1) * LANES]
        lane = lax.broadcasted_iota(I32, (b - a, LANES), 1)
        return jnp.where(lane < C_NOPE, kv, kr_ref[a:b, :].astype(kv.dtype)), kv

    outs = []
    for h in range(2):
        q = q_ref[:, h * LANES:(h + 1) * LANES]
        kd, vd = keys(lo, lo + tq, h)
        sd = jnp.where(col <= row, _dot_nt(q, kd) * MLA_SCALE, -jnp.inf)
        m = jnp.max(sd, axis=-1, keepdims=True)
        if n > 0:
            ka, va = keys(0, lo, h)
            sa = _dot_nt(q, ka) * MLA_SCALE
            m = jnp.maximum(m, jnp.max(sa, axis=-1, keepdims=True))
        pd = jnp.exp(sd - m)
        l = jnp.sum(pd, axis=-1, keepdims=True)
        acc = _dot(pd.astype(BF16), vd)
        if n > 0:
            pa = jnp.exp(sa - m)
            l = l + jnp.sum(pa, axis=-1, keepdims=True)
            acc = acc + _dot(pa.astype(BF16), va)
        outs.append(acc / l)
    lane_o = lax.broadcasted_iota(I32, (tq, LANES), 1)
    o_ref[...] = jnp.where(lane_o < C_V, pltpu.roll(outs[0], C_V, 1), outs[1]).astype(o_ref.dtype)


def _mla_prompt_kernel(q_ref, kv_ref, kr_ref, o_ref, *, tq, nq):
    i = pl.program_id(2)
    for n in range(nq):
        pl.when(i == n)(functools.partial(_mla_prompt_block, q_ref, kv_ref, kr_ref, o_ref, n, tq))


def _mla_prompt(q, kv, kr, *, B, S, tq=512):
    tq = min(tq, S)
    nq = S // tq
    return pl.pallas_call(
        functools.partial(_mla_prompt_kernel, tq=tq, nq=nq),
        grid=(B, C_HEADS // 2, nq),
        in_specs=[pl.BlockSpec((tq, 2 * LANES), lambda b, h, i: (b * nq + i, h)),
                  pl.BlockSpec((S, 2 * LANES), lambda b, h, i: (b, h)),
                  pl.BlockSpec((S, LANES), lambda b, h, i: (b, 0))],
        out_specs=pl.BlockSpec((tq, LANES), lambda b, h, i: (b * nq + i, h)),
        out_shape=jax.ShapeDtypeStruct((B * S, C_HEADS * C_V), BF16),
        compiler_params=_params("parallel", "parallel", "arbitrary"), name="mla_prompt",
    )(q, kv, kr)


def _order_key(sc):
    sc = jnp.where(sc == 0.0, 0.0, sc)
    bits = lax.bitcast_convert_type(sc, I32)
    return bits ^ ((bits >> 31) & 0x7FFFFFFF)


def _topk_search(count, k, n_idx_bits):
    kf = float(k)
    t0 = jnp.where(count(lambda key, idx: key >= 0) >= kf, 0, INT_MIN).astype(I32)

    def vbody(n, t):
        cand = t | jnp.left_shift(jnp.int32(1), 30 - n)
        return jnp.where(count(lambda key, idx: key >= cand) >= kf, cand, t)

    t = lax.fori_loop(0, 31, vbody, t0)
    need = kf - count(lambda key, idx: key > t)
    n_eq = count(lambda key, idx: key == t)

    def search():
        def ibody(n, x):
            cand = x | jnp.left_shift(jnp.int32(1), n_idx_bits - 1 - n)
            return jnp.where(count(lambda key, idx: (key == t) & (idx < cand)) < need, cand, x)

        return lax.fori_loop(0, n_idx_bits, ibody, jnp.zeros_like(t))

    tied = jnp.max(jnp.where((n_eq > need) & (t > INT_MIN), 1.0, 0.0)) > 0.5
    x = lax.cond(tied, search, lambda: jnp.full_like(t, (1 << n_idx_bits) - 1))
    return t, x


def _topk_mask(key, idx, k, axis, n_idx_bits):
    def count(pred):
        return jnp.sum(pred(key, idx).astype(F32), axis=axis, keepdims=True)

    t, x = _topk_search(count, k, n_idx_bits)
    return (key > t) | ((key == t) & (idx <= x))


def _dsa_prompt_kernel(iq_ref, ik_ref, iw_ref, aq_ref, ak_ref, avt_ref, o_ref,
                       key_ref, m_ref, l_ref, acc_ref, *, tq, topk):
    i = pl.program_id(1)
    S = ik_ref.shape[0]
    n_chunks = i + 1
    kofs = lax.broadcasted_iota(I32, (tq, tq), 0)
    qidx = i * tq + lax.broadcasted_iota(I32, (tq, tq), 1)

    def chunk_start(c):
        return pl.multiple_of(c * tq, tq)

    def score_chunk(c, carry):
        ik = ik_ref[pl.ds(chunk_start(c), tq), :].astype(BF16)
        sc = jnp.zeros((tq, tq), F32)
        for h in range(IDX_HEADS):
            s = _dot_nt(ik, iq_ref[:, h * LANES:(h + 1) * LANES])
            sc = sc + iw_ref[h:h + 1, :] * jnp.maximum(s, 0.0)
        key_ref[pl.ds(chunk_start(c), tq), :] = jnp.where(kofs + c * tq <= qidx, _order_key(sc), INT_MIN)
        return carry

    lax.fori_loop(0, n_chunks, score_chunk, 0)

    def count(pred):
        def body(c, acc):
            hit = pred(key_ref[pl.ds(chunk_start(c), tq), :], kofs + c * tq)
            return acc + jnp.sum(hit.astype(F32).reshape(tq // SUBLANES, SUBLANES, tq), axis=0)

        part = lax.fori_loop(0, n_chunks, body, jnp.zeros((SUBLANES, tq), F32))
        return jnp.sum(part, axis=0, keepdims=True)

    t, x = _topk_search(count, topk, max(1, (S - 1).bit_length()))

    m_ref[...] = jnp.full(m_ref.shape, NEG_BIG, F32)
    l_ref[...] = jnp.zeros(l_ref.shape, F32)
    acc_ref[...] = jnp.zeros(acc_ref.shape, F32)
    rep = A_HEADS // A_KV_HEADS

    def attend_chunk(c, carry):
        start = chunk_start(c)
        key = key_ref[pl.ds(start, tq), :]
        kidx = kofs + c * tq
        sel = ((key > t) | ((key == t) & (kidx <= x))) & (kidx <= qidx)
        ak = ak_ref[pl.ds(start, tq), :].astype(BF16)
        avt = avt_ref[0, :, pl.ds(start, tq)].astype(BF16)
        heads = range(A_HEADS)
        scale = A_HEAD_DIM ** -0.5
        s = [jnp.where(sel, _dot_nt(ak, aq_ref[:, h * LANES:(h + 1) * LANES]) * scale, NEG_BIG)
             for h in heads]
        m_old = m_ref[...]
        m_new = jnp.maximum(m_old, jnp.concatenate([jnp.max(s[h], axis=0, keepdims=True) for h in heads],
                                                   axis=0))
        corr = jnp.exp(m_old - m_new)
        p = [jnp.exp(s[h] - m_new[h:h + 1, :]) for h in heads]
        l_ref[...] = l_ref[...] * corr + jnp.concatenate(
            [jnp.sum(p[h], axis=0, keepdims=True) for h in heads], axis=0)
        pv = [_dot(avt[(h // rep) * A_HEAD_DIM:(h // rep + 1) * A_HEAD_DIM], p[h].astype(BF16))
              for h in heads]
        for h in heads:
            acc_ref[h] = acc_ref[h] * corr[h:h + 1, :] + pv[h]
        m_ref[...] = m_new
        return carry

    lax.fori_loop(0, n_chunks, attend_chunk, 0)
    outs = [acc_ref[h] / l_ref[h:h + 1, :] for h in range(A_HEADS)]
    o_ref[...] = jnp.concatenate(outs, axis=0).T.astype(o_ref.dtype)


def _dsa_prompt(iq, ik, iw_t, aq, ak, av_t, *, B, S, tq=256):
    tq = min(tq, S)
    nq = S // tq
    topk = min(TOPK_MAX, S // 4)
    W = A_HEADS * LANES
    return pl.pallas_call(
        functools.partial(_dsa_prompt_kernel, tq=tq, topk=topk),
        grid=(B, nq),
        scratch_shapes=[pltpu.VMEM((S, tq), I32), pltpu.VMEM((A_HEADS, tq), F32),
                        pltpu.VMEM((A_HEADS, tq), F32), pltpu.VMEM((A_HEADS, A_HEAD_DIM, tq), F32)],
        in_specs=[pl.BlockSpec((tq, W), lambda b, i: (b * nq + i, 0)),
                  pl.BlockSpec((S, LANES), lambda b, i: (b, 0)),
                  pl.BlockSpec((IDX_HEADS, tq), lambda b, i: (0, b * nq + i)),
                  pl.BlockSpec((tq, W), lambda b, i: (b * nq + i, 0)),
                  pl.BlockSpec((S, LANES), lambda b, i: (b, 0)),
                  pl.BlockSpec((1, LANES, S), lambda b, i: (b, 0, 0))],
        out_specs=pl.BlockSpec((tq, A_HEADS * A_HEAD_DIM), lambda b, i: (b * nq + i, 0)),
        out_shape=jax.ShapeDtypeStruct((B * S, A_HEADS * A_HEAD_DIM), BF16),
        compiler_params=_params("parallel", "arbitrary"), name="dsa_prompt",
    )(iq, ik, iw_t, aq, ak, av_t)


GDN_BETA_LANE = 8
GDN_DECAY_LANE = 16
GDN_TAIL = SUBLANES


def _softplus(x):
    return jnp.maximum(x, 0.0) + jnp.log1p(jnp.exp(-jnp.abs(x)))


def _silu(x):
    return x * jax.nn.sigmoid(x)


def _dot3(a, b, nt=False):
    f = _dot_nt if nt else _dot
    a1 = a.astype(BF16)
    a2 = (a - a1.astype(F32)).astype(BF16)
    b1 = b.astype(BF16)
    b2 = (b - b1.astype(F32)).astype(BF16)
    return f(a1, b1) + (f(a1, b2) + f(a2, b1))


def _gdn_kernel(xin_ref, z_ref, misc_ref, misct_ref, tail0_ref, s0_ref, cw_ref, alog_ref, dtb_ref,
                alogt_ref, dtbt_ref, ng_ref, o_ref, tail_ref, sout_ref, xp_ref, s_ref, *, C, NB):
    c = pl.program_id(1)
    NP = B_HEADS // 2

    @pl.when(c == 0)
    def _():
        xp_ref[:, 0:GDN_TAIL, :] = tail0_ref[...]
        s_ref[...] = s0_ref[...]

    @pl.when(c > 0)
    def _():
        xp_ref[:, 0:GDN_TAIL, :] = xp_ref[:, C:C + GDN_TAIL, :]

    xp_ref[:, GDN_TAIL:GDN_TAIL + C, :] = xin_ref[...]
    tail_ref[...] = xp_ref[:, C:C + GDN_TAIL, :]
    base = GDN_TAIL - (CONV_WIDTH - 1)
    ti = lax.broadcasted_iota(I32, (C, C), 0)
    tj = lax.broadcasted_iota(I32, (C, C), 1)
    low = (tj <= ti).astype(BF16)
    upp = (ti <= tj).astype(BF16)

    first = lax.broadcasted_iota(I32, (C, LANES), 1) < B_HEAD_DIM
    li = lax.broadcasted_iota(I32, (LANES, LANES), 0)
    lj = lax.broadcasted_iota(I32, (LANES, LANES), 1)
    blockdiag = (li < B_HEAD_DIM) == (lj < B_HEAD_DIM)
    ones_bd = blockdiag.astype(BF16)
    own_lanes = ((lax.broadcasted_iota(I32, (2 * C, LANES), 0) < C)
                 == (lax.broadcasted_iota(I32, (2 * C, LANES), 1) < B_HEAD_DIM))
    ri = lax.broadcasted_iota(I32, (2 * C, 2 * C), 0)
    rj = lax.broadcasted_iota(I32, (2 * C, 2 * C), 1)
    same_head = (ri < C) == (rj < C)
    incl = same_head & (rj <= ri)
    strict = same_head & (rj < ri)
    eye = (ri == rj).astype(F32)

    def pair(col0, col1):
        return jnp.where(first, col0, col1)

    def halves(x):
        return pair(x[:C, :LANES], x[C:, :LANES])

    def twice(x):
        return jnp.concatenate([x, x], axis=0)

    st = []
    for bb, p in [(bb, p) for bb in range(NB) for p in range(NP)]:
        if p == 0:
            y = xp_ref[bb, base:base + C, :] * cw_ref[0:1, :]
            for i in range(1, CONV_WIDTH):
                y = y + xp_ref[bb, base + i:base + i + C, :] * cw_ref[i:i + 1, :]
            xc = _silu(y)
            misc = misc_ref[bb]
            beta = jax.nn.sigmoid(misc)
            g = -jnp.exp(alog_ref[...]) * _softplus(misc + dtb_ref[...])
            gt = -jnp.exp(alogt_ref[...]) * _softplus(misct_ref[bb, 0] + dtbt_ref[...])
            g1, g2, g3 = _split3(g)
            gc = _dot(low, g1) + (_dot(low, g2) + _dot(low, g3))
            t1, t2, t3 = _split3(gt)
            gct = _dot(t1, upp) + (_dot(t2, upp) + _dot(t3, upp))
        sl = slice(p * LANES, (p + 1) * LANES)
        q2 = xc[:, sl]
        k2 = xc[:, GDN_QK + p * LANES:GDN_QK + (p + 1) * LANES]
        v2 = xc[:, 2 * GDN_QK + p * LANES:2 * GDN_QK + (p + 1) * LANES]
        q2 = q2 * lax.rsqrt(_dot_exact_rhs(q2 * q2, ones_bd) + EPS) * (B_HEAD_DIM ** -0.5)
        k2 = k2 * lax.rsqrt(_dot_exact_rhs(k2 * k2, ones_bd) + EPS)
        bcol = [beta[:, GDN_BETA_LANE + 2 * p + e:GDN_BETA_LANE + 2 * p + e + 1] for e in range(2)]
        gcol = [gc[:, GDN_DECAY_LANE + 2 * p + e:GDN_DECAY_LANE + 2 * p + e + 1] for e in range(2)]
        glast = [gcol[e][C - 1:C, :] for e in range(2)]
        beta2 = pair(bcol[0], bcol[1])
        gam2 = jnp.exp(pair(gcol[0], gcol[1]))
        kd2 = k2 * jnp.exp(pair(glast[0] - gcol[0], glast[1] - gcol[1]))
        gl2 = jnp.exp(jnp.where(first[0:1, :], glast[0], glast[1]))
        rhs = twice(jnp.concatenate([v2 * beta2, k2 * (beta2 * gam2)], axis=1))
        grow = jnp.concatenate([gct[GDN_DECAY_LANE + 2 * p + e:GDN_DECAY_LANE + 2 * p + e + 1, :]
                                for e in range(2)], axis=1)
        decay = jnp.exp(jnp.where(incl, jnp.concatenate(gcol, axis=0) - grow, -jnp.inf))
        kx = jnp.where(own_lanes, twice(k2), 0.0)
        qx = jnp.where(own_lanes, twice(q2), 0.0)
        a = jnp.where(strict, jnp.concatenate(bcol, axis=0) * _dot3(kx, kx, nt=True) * decay, 0.0)
        aqk = _dot_nt(qx.astype(BF16), kx.astype(BF16)) * decay
        st.append(dict(bb=bb, p=p, sl=sl, q2=q2, gam2=gam2, kd2=kd2, gl2=gl2, rhs=rhs, aqk=aqk, a=a))

    tinv = [eye - d["a"] for d in st]
    xk = [d["a"] for d in st]
    span = 2
    while span < C:
        xk = [_dot3(x, x) for x in xk]
        tinv = [t + _dot3(t, x) for t, x in zip(tinv, xk)]
        span *= 2
    sols = [_dot3(t, d["rhs"]) for t, d in zip(tinv, st)]

    for d, sol in zip(st, sols):
        bb, p = d["bb"], d["p"]
        u2 = halves(sol[:, :LANES])
        w2 = halves(sol[:, LANES:])
        s_old = s_ref[bb, p]
        sb = s_old.astype(BF16)
        delta = u2 - _dot(w2.astype(BF16), sb)
        db = delta.astype(BF16)
        o2 = _dot((d["q2"] * d["gam2"]).astype(BF16), sb) + halves(_dot(d["aqk"].astype(BF16), twice(db)))
        upd = lax.dot_general(d["kd2"].astype(BF16), db, (((0,), (0,)), ((), ())),
                              preferred_element_type=F32)
        s_new = s_old * d["gl2"] + jnp.where(blockdiag, upd, 0.0)
        s_ref[bb, p] = s_new
        sout_ref[bb, p] = s_new
        ms = _dot_exact_rhs(o2 * o2, ones_bd) * (1.0 / B_HEAD_DIM)
        o_ref[bb, :, d["sl"]] = (o2 * lax.rsqrt(ms + EPS) * ng_ref[...]
                                 * _silu(z_ref[bb, :, d["sl"]])).astype(o_ref.dtype)


GDN_BATCH_PER_STEP = 2


def _gdn(xin, z, misc, tail0, s0, conv_w, a_log, dt_bias, norm_g, *, B, L):
    C = math.gcd(L, GDN_CHUNK)
    n = L // C
    M = B * L
    NP = B_HEADS // 2
    NB = math.gcd(B, GDN_BATCH_PER_STEP)
    nrow = GDN_DECAY_LANE + B_HEADS
    misc_t = jnp.swapaxes(misc[:, :nrow].reshape(B, n, C, nrow), 2, 3)
    xin, z, misc = (a.reshape(B, L, a.shape[1]) for a in (xin, z, misc))
    lane_vec = lambda v: jnp.zeros((1, LANES), F32).at[0, GDN_DECAY_LANE:nrow].set(v)
    col_vec = lambda v: jnp.zeros((nrow, 1), F32).at[GDN_DECAY_LANE:, 0].set(v)
    cw = jnp.pad(conv_w, ((0, SUBLANES - CONV_WIDTH), (0, 0)))
    ng = jnp.tile(norm_g.reshape(1, B_HEAD_DIM), (1, 2))
    full = lambda shape: pl.BlockSpec(shape, lambda b, c: (0,) * len(shape))
    o, tail, state = pl.pallas_call(
        functools.partial(_gdn_kernel, C=C, NB=NB),
        grid=(B // NB, n),
        in_specs=[pl.BlockSpec((NB, C, GDN_CONV_DIM), lambda b, c: (b, c, 0)),
                  pl.BlockSpec((NB, C, GDN_QK), lambda b, c: (b, c, 0)),
                  pl.BlockSpec((NB, C, LANES), lambda b, c: (b, c, 0)),
                  pl.BlockSpec((NB, 1, nrow, C), lambda b, c: (b, c, 0, 0)),
                  pl.BlockSpec((NB, GDN_TAIL, GDN_CONV_DIM), lambda b, c: (b, 0, 0)),
                  pl.BlockSpec((NB, NP, LANES, LANES), lambda b, c: (b, 0, 0, 0)),
                  full((SUBLANES, GDN_CONV_DIM)), full((1, LANES)), full((1, LANES)),
                  full((nrow, 1)), full((nrow, 1)), full((1, LANES))],
        out_specs=[pl.BlockSpec((NB, C, GDN_QK), lambda b, c: (b, c, 0)),
                   pl.BlockSpec((NB, GDN_TAIL, GDN_CONV_DIM), lambda b, c: (b, 0, 0)),
                   pl.BlockSpec((NB, NP, LANES, LANES), lambda b, c: (b, 0, 0, 0))],
        out_shape=[jax.ShapeDtypeStruct((B, L, GDN_QK), BF16),
                   jax.ShapeDtypeStruct((B, GDN_TAIL, GDN_CONV_DIM), F32),
                   jax.ShapeDtypeStruct((B, NP, LANES, LANES), F32)],
        scratch_shapes=[pltpu.VMEM((NB, C + GDN_TAIL, GDN_CONV_DIM), F32),
                        pltpu.VMEM((NB, NP, LANES, LANES), F32)],
        compiler_params=_params("arbitrary", "arbitrary"), name="gdn",
    )(xin, z, misc, misc_t, tail0, s0, cw, lane_vec(a_log), lane_vec(dt_bias),
      col_vec(a_log), col_vec(dt_bias), ng)
    return o.reshape(M, GDN_QK), tail, state


def _state_to_blockdiag(s):
    B = s.shape[0]
    d = B_HEAD_DIM
    s = s.reshape(B, B_HEADS // 2, 2, d, d)
    z = jnp.zeros_like(s[:, :, 0])
    top = jnp.concatenate([s[:, :, 0], z], axis=-1)
    bot = jnp.concatenate([z, s[:, :, 1]], axis=-1)
    return jnp.concatenate([top, bot], axis=-2)


def _state_from_blockdiag(s):
    d = B_HEAD_DIM
    return jnp.stack([s[:, :, :d, :d], s[:, :, d:, d:]], axis=2).reshape(s.shape[0], B_HEADS, d, d)


PAGES_PER_STEP = 16
SCORE_PAGES_PER_STEP = 32
XATTN_ROWS_PER_STEP = 8
PAGES_PER_BLOCK = 4
SELECT_ROWS = 64


def _page_specs(block, layer, n):
    nd = len(block)

    def spec(p):
        return pl.BlockSpec(block, lambda b, j, pt, p=p: (pt[b, j * n + p], layer) + (0,) * (nd - 2))

    return [spec(p) for p in range(n)]


def _stack_heads(x, n_heads, width=LANES):
    return jnp.concatenate([x[:, h * width:(h + 1) * width] for h in range(n_heads)], axis=0)


def _pad_rows(x, rows):
    return jnp.concatenate([x, jnp.zeros((rows - x.shape[0], x.shape[1]), x.dtype)], axis=0)


def _dsa_s_score_kernel(pt_ref, iq_ref, misc_ref, *refs, n_pg, T):
    pages, o_ref = refs[:n_pg], refs[n_pg]
    q = _stack_heads(iq_ref[...], IDX_HEADS)[:, :IDX_DIM].astype(BF16)
    misc = misc_ref[...]
    w = jnp.concatenate([misc[:, h:h + 1] for h in range(IDX_HEADS)], axis=0)
    keys_t = jnp.concatenate([pages[p][0, 0].astype(BF16) for p in range(n_pg)], axis=1)
    r = jnp.maximum(_dot(q, keys_t), 0.0) * w
    sc = r[0:T]
    for h in range(1, IDX_HEADS):
        sc = sc + r[h * T:(h + 1) * T]
    o_ref[0] = sc


def _dsa_s_select_kernel(sp_ref, iq_ref, ikn_ref, misc_ref, bp_ref, bn_ref, *, G, T, topk):
    past = sp_ref.shape[2]
    R = G * T
    misc = misc_ref[...]
    iq = iq_ref[...]
    ikn = _pad_rows(ikn_ref[...], LANES).astype(BF16)
    sn = jnp.zeros((R, LANES), F32)
    for h in range(IDX_HEADS):
        s = _dot_nt(iq[:, h * LANES:(h + 1) * LANES].astype(BF16), ikn)
        sn = sn + misc[:, h:h + 1] * jnp.maximum(s, 0.0)
    row = lax.broadcasted_iota(I32, (R, LANES), 0)
    lane = lax.broadcasted_iota(I32, (R, LANES), 1)
    visible = (lane // T == row // T) & (lane % T <= row % T)
    key_new = jnp.where(visible, _order_key(sn), INT_MIN)
    key = jnp.concatenate([_order_key(sp_ref[...].reshape(R, past)), key_new], axis=1)
    idx = lax.broadcasted_iota(I32, key.shape, 1)
    sel = _topk_mask(key, idx, topk, 1, (past + LANES - 1).bit_length())
    bias = jnp.where(sel, 0.0, NEG_BIG)
    bp_ref[...] = bias[:, :past].reshape(G, T, past)
    bn_ref[...] = jnp.where(visible, bias[:, past:], NEG_BIG).reshape(G, T, LANES)


def _online_softmax_step(s, v, m_ref, l_ref, acc_ref, v_transposed=False):
    m_old = m_ref[...]
    m_new = jnp.maximum(m_old, jnp.max(s, axis=-1, keepdims=True))
    corr = jnp.exp(m_old - m_new)
    p = jnp.exp(s - m_new)
    l_ref[...] = l_ref[...] * corr + jnp.sum(p, axis=-1, keepdims=True)
    pv = _dot_nt(p.astype(BF16), v) if v_transposed else _dot(p.astype(BF16), v)
    acc_ref[...] = acc_ref[...] * corr + pv
    m_ref[...] = m_new


def _softmax_blocks_step(blocks, m_ref, l_ref, acc_ref, v_transposed=False):
    parts = []
    for s, v in blocks:
        m_b = jnp.max(s, axis=-1, keepdims=True)
        p = jnp.exp(s - m_b)
        pv = _dot_nt(p.astype(BF16), v) if v_transposed else _dot(p.astype(BF16), v)
        parts.append((m_b, jnp.sum(p, axis=-1, keepdims=True), pv))
    m_old = m_ref[...]
    m_new = m_old
    for m_b, _, _ in parts:
        m_new = jnp.maximum(m_new, m_b)
    corr = jnp.exp(m_old - m_new)
    l = l_ref[...] * corr
    acc = acc_ref[...] * corr
    for m_b, l_b, pv in parts:
        w = jnp.exp(m_b - m_new)
        l = l + l_b * w
        acc = acc + pv * w
    l_ref[...] = l
    acc_ref[...] = acc
    m_ref[...] = m_new


def _dsa_s_attend_kernel(pt_ref, aq_ref, akn_ref, avn_ref, bp_ref, bn_ref, *refs, n_pg, T):
    kpages, vpages = refs[:n_pg], refs[n_pg:2 * n_pg]
    o_ref, q_ref, m_ref, l_ref, acc_ref = refs[2 * n_pg:]
    j = pl.program_id(1)
    scale = A_HEAD_DIM ** -0.5

    @pl.when(j == 0)
    def _():
        q_ref[...] = _stack_heads(aq_ref[...], A_HEADS).astype(BF16)
        m_ref[...] = jnp.full(m_ref.shape, NEG_BIG, F32)
        l_ref[...] = jnp.zeros(l_ref.shape, F32)
        acc_ref[...] = jnp.zeros(acc_ref.shape, F32)
        kn = _pad_rows(akn_ref[...], LANES).astype(BF16)
        vn = _pad_rows(avn_ref[...], LANES).astype(BF16)
        s = _dot_nt(q_ref[...], kn) * scale + jnp.concatenate([bn_ref[0]] * A_HEADS, axis=0)
        _online_softmax_step(s, vn, m_ref, l_ref, acc_ref)

    q = q_ref[...]
    blocks = []
    for p0 in range(0, n_pg, PAGES_PER_BLOCK):
        group = range(p0, min(p0 + PAGES_PER_BLOCK, n_pg))
        k_t = jnp.concatenate([kpages[p][0, 0].astype(BF16) for p in group], axis=1)
        v_t = jnp.concatenate([vpages[p][0, 0].astype(BF16) for p in group], axis=1)
        ps = kpages[0].shape[3]
        bias = bp_ref[0, :, p0 * ps:(group[-1] + 1) * ps]
        blocks.append((_dot(q, k_t) * scale + jnp.concatenate([bias] * A_HEADS, axis=0), v_t))
    _softmax_blocks_step(blocks, m_ref, l_ref, acc_ref, v_transposed=True)

    @pl.when(j == pl.num_programs(1) - 1)
    def _():
        o = acc_ref[...] / l_ref[...]
        lane = lax.broadcasted_iota(I32, (T, LANES), 1)
        rep = A_HEADS // A_KV_HEADS
        for hp in range(A_HEADS // 2):
            a = o[(2 * hp) * T:(2 * hp + 1) * T]
            b = o[(2 * hp + 1) * T:(2 * hp + 2) * T]
            if (2 * hp) // rep == 0:
                b = pltpu.roll(b, A_HEAD_DIM, 1)
            else:
                a = pltpu.roll(a, A_HEAD_DIM, 1)
            o_ref[:, hp * LANES:(hp + 1) * LANES] = jnp.where(lane < A_HEAD_DIM, a, b).astype(o_ref.dtype)


def _dsa_sample(iq, ikn, misc, aq, akn, avn, pool_k, pool_v, pool_idx, page_table, layer, *, Bd, T):
    n_pages = page_table.shape[1]
    page = pool_idx.shape[2]
    past = n_pages * page
    n_pg = math.gcd(PAGES_PER_STEP, n_pages)
    npg = n_pages // n_pg
    topk = min(TOPK_MAX, (past + T) // 4)
    W = A_HEADS * LANES
    kv_dim = A_KV_HEADS * A_HEAD_DIM
    pk = jnp.transpose(pool_k, (0, 1, 3, 4, 2)).reshape(pool_k.shape[:2] + (kv_dim, page))
    pv = jnp.transpose(pool_v, (0, 1, 3, 4, 2)).reshape(pool_v.shape[:2] + (kv_dim, page))
    pi = jnp.swapaxes(pool_idx, 2, 3)
    row = lambda w: pl.BlockSpec((T, w), lambda b, j, pt: (b, 0))

    n_sc = math.gcd(SCORE_PAGES_PER_STEP, n_pages)
    scores = pl.pallas_call(
        functools.partial(_dsa_s_score_kernel, n_pg=n_sc, T=T),
        grid_spec=pltpu.PrefetchScalarGridSpec(
            num_scalar_prefetch=1, grid=(Bd, n_pages // n_sc),
            in_specs=[row(W), row(LANES)] + _page_specs((1, 1, IDX_DIM, page), layer, n_sc),
            out_specs=pl.BlockSpec((1, T, n_sc * page), lambda b, j, pt: (b, 0, j))),
        out_shape=jax.ShapeDtypeStruct((Bd, T, past), F32),
        compiler_params=_params("parallel", "arbitrary"), name="dsa_s_score",
    )(page_table, iq, misc, *([pi] * n_sc))

    G = math.gcd(Bd, max(1, SELECT_ROWS // T))
    assert G * T <= LANES
    bias_p, bias_n = pl.pallas_call(
        functools.partial(_dsa_s_select_kernel, G=G, T=T, topk=topk),
        grid=(Bd // G,),
        in_specs=[pl.BlockSpec((G, T, past), lambda b: (b, 0, 0)),
                  pl.BlockSpec((G * T, W), lambda b: (b, 0)),
                  pl.BlockSpec((G * T, LANES), lambda b: (b, 0)),
                  pl.BlockSpec((G * T, LANES), lambda b: (b, 0))],
        out_specs=[pl.BlockSpec((G, T, past), lambda b: (b, 0, 0)),
                   pl.BlockSpec((G, T, LANES), lambda b: (b, 0, 0))],
        out_shape=[jax.ShapeDtypeStruct((Bd, T, past), F32),
                   jax.ShapeDtypeStruct((Bd, T, LANES), F32)],
        compiler_params=_params("parallel"), name="dsa_s_select",
    )(scores, iq, ikn, misc)

    group_rows = pl.BlockSpec((G * T, LANES), lambda b, j, pt: (b // G, 0))
    return pl.pallas_call(
        functools.partial(_dsa_s_attend_kernel, n_pg=n_pg, T=T),
        grid_spec=pltpu.PrefetchScalarGridSpec(
            num_scalar_prefetch=1, grid=(Bd, npg),
            in_specs=[row(W), group_rows, group_rows,
                      pl.BlockSpec((1, T, n_pg * page), lambda b, j, pt: (b, 0, j)),
                      pl.BlockSpec((1, T, LANES), lambda b, j, pt: (b, 0, 0))]
            + _page_specs((1, 1, kv_dim, page), layer, n_pg) * 2,
            out_specs=pl.BlockSpec((T, A_HEADS * A_HEAD_DIM), lambda b, j, pt: (b, 0)),
            scratch_shapes=[pltpu.VMEM((A_HEADS * T, LANES), BF16),
                            pltpu.VMEM((A_HEADS * T, 1), F32),
                            pltpu.VMEM((A_HEADS * T, 1), F32),
                            pltpu.VMEM((A_HEADS * T, LANES), F32)]),
        out_shape=jax.ShapeDtypeStruct((Bd * T, A_HEADS * A_HEAD_DIM), BF16),
        compiler_params=_params("parallel", "arbitrary"), name="dsa_s_attend",
    )(page_table, aq, akn, avn, bias_p, bias_n, *([pk] * n_pg), *([pv] * n_pg))


def _head_matmul_kernel(x_ref, w_ref, o_ref):
    o_ref[0] = _dot(x_ref[...].astype(BF16), w_ref[0])


def _mla_s_attend_kernel(pt_ref, ql_ref, q_ref, latn_ref, krn_ref, *refs, n_pg, T):
    lpages, rpages = refs[:n_pg], refs[n_pg:2 * n_pg]
    o_ref, qlat_ref, qr_ref, m_ref, l_ref, acc_ref = refs[2 * n_pg:]
    j = pl.program_id(1)
    R = C_HEADS * T
    lo = C_NOPE

    @pl.when(j == 0)
    def _():
        qlat_ref[...] = ql_ref[...].reshape(R, C_KV_LORA).astype(BF16)
        q = q_ref[...]
        qr_ref[...] = jnp.concatenate(
            [q[:, h * LANES + lo:h * LANES + lo + C_ROPE] for h in range(C_HEADS)], axis=0).astype(BF16)
        m_ref[...] = jnp.full(m_ref.shape, NEG_BIG, F32)
        l_ref[...] = jnp.zeros(l_ref.shape, F32)
        acc_ref[...] = jnp.zeros(acc_ref.shape, F32)
        latn = _pad_rows(latn_ref[...], LANES).astype(BF16)
        krn = _pad_rows(krn_ref[...][:, lo:lo + C_ROPE], LANES).astype(BF16)
        s = (_dot_nt(qlat_ref[...], latn) + _dot_nt(qr_ref[...], krn)) * MLA_SCALE
        t_row = lax.broadcasted_iota(I32, (R, LANES), 0) % T
        n_key = lax.broadcasted_iota(I32, (R, LANES), 1)
        s = jnp.where(n_key <= t_row, s, NEG_BIG)
        _online_softmax_step(s, latn, m_ref, l_ref, acc_ref)

    ql = qlat_ref[...]
    qr = qr_ref[...]
    blocks = []
    for p0 in range(0, n_pg, PAGES_PER_BLOCK):
        group = range(p0, min(p0 + PAGES_PER_BLOCK, n_pg))
        lat = jnp.concatenate([lpages[p][0, 0].astype(BF16) for p in group], axis=0)
        kr_t = jnp.concatenate([rpages[p][0, 0].astype(BF16) for p in group], axis=1)
        blocks.append(((_dot_nt(ql, lat) + _dot(qr, kr_t)) * MLA_SCALE, lat))
    _softmax_blocks_step(blocks, m_ref, l_ref, acc_ref)

    @pl.when(j == pl.num_programs(1) - 1)
    def _():
        o_ref[...] = (acc_ref[...] / l_ref[...]).reshape(C_HEADS, T, C_KV_LORA)


def _pair_matmul_kernel(a_ref, b_ref, wa_ref, wb_ref, o_ref):
    o_ref[...] = (_dot(a_ref[0].astype(BF16), wa_ref[0])
                  + _dot(b_ref[0].astype(BF16), wb_ref[0])).astype(o_ref.dtype)


def _mla_sample(q, latn, krn, w_ukv, pool_lat, pool_kr, page_table, layer, *, Bd, T):
    Ms = Bd * T
    n_pages = page_table.shape[1]
    page = pool_lat.shape[2]
    n_pg = math.gcd(PAGES_PER_STEP, n_pages)
    npg = n_pages // n_pg
    w_uk_t = jnp.pad(jnp.transpose(w_ukv[..., :C_NOPE], (1, 2, 0)),
                     ((0, 0), (0, LANES - C_NOPE), (0, 0))).astype(BF16)
    wv = jnp.transpose(w_ukv[..., C_NOPE:], (1, 0, 2))
    even = (jnp.arange(C_HEADS) % 2 == 0)[:, None, None]
    z = jnp.zeros_like(wv)
    w_uv = jnp.concatenate([jnp.where(even, wv, z), jnp.where(even, z, wv)], axis=-1).astype(BF16)

    q_lat = pl.pallas_call(
        _head_matmul_kernel, grid=(C_HEADS,),
        in_specs=[pl.BlockSpec((Ms, LANES), lambda h: (0, h)),
                  pl.BlockSpec((1, LANES, C_KV_LORA), lambda h: (h, 0, 0))],
        out_specs=pl.BlockSpec((1, Ms, C_KV_LORA), lambda h: (h, 0, 0)),
        out_shape=jax.ShapeDtypeStruct((C_HEADS, Ms, C_KV_LORA), F32),
        compiler_params=_params("parallel"), name="mla_s_qlat",
    )(q, w_uk_t)

    R = C_HEADS * T
    row = lambda w: pl.BlockSpec((T, w), lambda b, j, pt: (b, 0))
    o_lat = pl.pallas_call(
        functools.partial(_mla_s_attend_kernel, n_pg=n_pg, T=T),
        grid_spec=pltpu.PrefetchScalarGridSpec(
            num_scalar_prefetch=1, grid=(Bd, npg),
            in_specs=[pl.BlockSpec((C_HEADS, T, C_KV_LORA), lambda b, j, pt: (0, b, 0)),
                      row(C_HEADS * LANES), row(C_KV_LORA), row(LANES)]
            + _page_specs((1, 1, page, C_KV_LORA), layer, n_pg)
            + _page_specs((1, 1, C_ROPE, page), layer, n_pg),
            out_specs=pl.BlockSpec((C_HEADS, T, C_KV_LORA), lambda b, j, pt: (0, b, 0)),
            scratch_shapes=[pltpu.VMEM((R, C_KV_LORA), BF16), pltpu.VMEM((R, C_ROPE), BF16),
                            pltpu.VMEM((R, 1), F32), pltpu.VMEM((R, 1), F32),
                            pltpu.VMEM((R, C_KV_LORA), F32)]),
        out_shape=jax.ShapeDtypeStruct((C_HEADS, Ms, C_KV_LORA), F32),
        compiler_params=_params("parallel", "arbitrary"), name="mla_s_attend",
    )(page_table, q_lat, q, latn, krn, *([pool_lat] * n_pg), *([jnp.swapaxes(pool_kr, 2, 3)] * n_pg))

    return pl.pallas_call(
        _pair_matmul_kernel, grid=(C_HEADS // 2,),
        in_specs=[pl.BlockSpec((1, Ms, C_KV_LORA), lambda h: (2 * h, 0, 0)),
                  pl.BlockSpec((1, Ms, C_KV_LORA), lambda h: (2 * h + 1, 0, 0)),
                  pl.BlockSpec((1, C_KV_LORA, LANES), lambda h: (2 * h, 0, 0)),
                  pl.BlockSpec((1, C_KV_LORA, LANES), lambda h: (2 * h + 1, 0, 0))],
        out_specs=pl.BlockSpec((Ms, LANES), lambda h: (0, h)),
        out_shape=jax.ShapeDtypeStruct((Ms, C_HEADS * C_V), BF16),
        compiler_params=_params("parallel"), name="mla_s_out",
    )(o_lat, o_lat, w_uv, w_uv)


def _mla_weights(w_down, w_uq, w_ukv):
    D = w_down.shape[0]
    kr_cols = w_down[:, C_Q_LORA + C_KV_LORA:]
    wd = jnp.concatenate([w_down[:, :C_Q_LORA + C_KV_LORA], jnp.zeros((D, C_NOPE), F32), kr_cols,
                          jnp.zeros((D, LANES - C_NOPE - C_ROPE), F32)], axis=1)
    wq = jnp.pad(w_uq, ((0, 0), (0, 0), (0, LANES - C_NOPE - C_ROPE))).reshape(C_Q_LORA, C_HEADS * LANES)
    wkv = w_ukv.reshape(C_KV_LORA, C_HEADS * LANES)
    return wd.astype(BF16), wq.astype(BF16), wkv.astype(BF16)


def _mla_project(x, g_mix, wd, wq, wkv, g_q, g_kv, rope, tm, qdtype):
    cq, ckv, kr = _proj(x, wd, ((C_Q_LORA, False, F32), (C_KV_LORA, False, F32), (LANES, True, F32)),
                        gain=g_mix, rope=rope, tm=tm, name="mla_down")
    (q,) = _proj(cq, wq, ((C_HEADS * LANES, True, qdtype),), gain=g_q, rope=rope, tm=tm, name="mla_uq")
    kv, lat = _proj(ckv, wkv, ((C_HEADS * LANES, False, BF16),), gain=g_kv, emit_norm=True, tm=tm,
                    name="mla_ukv")
    return q, kv, lat, kr


def _even_weights(w_in):
    D = w_in.shape[0]
    sizes = (A_HEADS * A_HEAD_DIM, A_KV_HEADS * A_HEAD_DIM, A_KV_HEADS * A_HEAD_DIM,
             IDX_HEADS * IDX_DIM, IDX_DIM, IDX_HEADS, GDN_QK, GDN_QK, GDN_QK, GDN_QK, B_HEADS, B_HEADS)
    parts, c = [], 0
    for s in sizes:
        parts.append(w_in[:, c:c + s])
        c += s
    aq, ak, av, iq, ik, iw, bq, bk, bv, bz, bb, ba = parts
    zero = jnp.zeros((D, A_HEADS, A_HEAD_DIM), F32)
    aq = aq.reshape(D, A_HEADS, A_HEAD_DIM)
    in_g0 = (jnp.arange(A_HEADS) < A_HEADS // A_KV_HEADS)[None, :, None]
    aq128 = jnp.concatenate([jnp.where(in_g0, aq, zero), jnp.where(in_g0, zero, aq)], axis=-1)
    iq128 = jnp.concatenate([iq.reshape(D, IDX_HEADS, IDX_DIM), zero], axis=-1)
    ik128 = jnp.pad(ik, ((0, 0), (0, LANES - IDX_DIM)))
    misc = jnp.pad(jnp.concatenate([iw, bb, ba], axis=1), ((0, 0), (0, LANES - IDX_HEADS - 2 * B_HEADS)))
    w = jnp.concatenate([aq128.reshape(D, -1), ak, iq128.reshape(D, -1), ik128, av, misc, bq, bk, bv, bz],
                        axis=1)
    misc_col = A_HEADS * LANES + LANES + IDX_HEADS * LANES + LANES + LANES
    scale = jnp.ones((w.shape[1],), F32).at[misc_col:misc_col + IDX_HEADS].set(IDX_W_SCALE)
    return w.astype(BF16), scale


def _even_groups(qdtype, kdtype, t_rows):
    return ((A_HEADS * LANES, True, qdtype), (LANES, True, kdtype, t_rows),
            (IDX_HEADS * LANES, True, qdtype), (LANES, True, kdtype, t_rows),
            (LANES, False, None if t_rows else F32, t_rows), (LANES, False, F32),
            (GDN_CONV_DIM, False, F32), (GDN_QK, False, F32))


def _tile_rows(t, rows):
    return jnp.tile(t, (rows // t.shape[0], 1))


def kernel(x_prompt, x_sample, cache_a_k, cache_a_v, cache_a_idx, state_b_ssm, state_b_conv,
           cache_c_latent, cache_c_krope, cache_mem_k, cache_mem_v, page_table, mem_prompt,
           g_mix, g_cross, g_mem, g_mlp, g_final,
           w_in_even, gdn_conv_w, gdn_a_log, gdn_dt_bias, gdn_norm_g, w_out_even,
           w_down_odd, g_q_lora, g_kv_lora, w_uq, w_ukv, w_out_odd,
           w_xq, w_xk, w_xv, w_xo, w_ff1, w_ff2):
    Bp, S, D = x_prompt.shape
    Bd, T, _ = x_sample.shape
    depth = g_mix.shape[0]
    Mp, Ms = Bp * S, Bd * T
    past_len = page_table.shape[1] * cache_a_k.shape[2]
    pos_p = jnp.arange(S, dtype=jnp.int32)
    pos_s = past_len + jnp.arange(T, dtype=jnp.int32)
    tm_p = min(256, S)
    tm_s = min(256, Ms)

    def rope_pair(dim, lo, hi):
        cp, sap, sbp, half = _rope_tables(pos_p, dim, lo, hi)
        cs, sas, sbs, _ = _rope_tables(pos_s, dim, lo, hi)
        return ((cp, sap, sbp, half, S // tm_p),
                (_tile_rows(cs, tm_s), _tile_rows(sas, tm_s), _tile_rows(sbs, tm_s), half, 1))

    rope_even_p, rope_even_s = rope_pair(A_HEAD_DIM, 0, LANES)
    rope_odd_p, rope_odd_s = rope_pair(C_ROPE, C_NOPE, C_NOPE + C_ROPE)

    xp = x_prompt.reshape(Mp, D)
    xs = x_sample.reshape(Ms, D)
    mem = mem_prompt.reshape(Bp * N_MEM, D)
    res = ((D, False, F32),)
    pa_k, pa_v, pa_i, pb_s, pb_c, pc_l, pc_r, pm_k, pm_v = [], [], [], [], [], [], [], [], []
    sa_k, sa_v, sa_i, sb_s, sb_c, sc_l, sc_r = [], [], [], [], [], [], []
    for li in range(depth):
        if li % 2 == 0:
            e = li // 2
            w_even, cscale = _even_weights(w_in_even[e])
            w_out = w_out_even[e].astype(BF16)
            gdn_w = (gdn_conv_w[e], gdn_a_log[e], gdn_dt_bias[e], gdn_norm_g[e])
            aq, ak, iq, ik, misc, conv_in, bz, ak_t, ik_t, av_t = _proj(
                xp, w_even, _even_groups(BF16, BF16, S), gain=g_mix[li], rope=rope_even_p,
                colscale=cscale, tm=tm_p, name="even_in")
            a_out = _dsa_prompt(iq, ik, misc[:, :IDX_HEADS].T, aq, ak, av_t, B=Bp, S=S)
            b_out, tail, sbd = _gdn(
                conv_in, bz, misc, jnp.zeros((Bp, GDN_TAIL, GDN_CONV_DIM), F32),
                jnp.zeros((Bp, B_HEADS // 2, LANES, LANES), F32), *gdn_w, B=Bp, L=S)
            (xp,) = _proj(jnp.concatenate([a_out, b_out], axis=1), w_out, res, residual=xp, tm=tm_p,
                          name="even_out")
            from_t = lambda a: jnp.transpose(a.reshape(Bp, A_KV_HEADS, A_HEAD_DIM, S), (0, 3, 1, 2))
            pa_k.append(from_t(ak_t))
            pa_v.append(from_t(av_t))
            pa_i.append(jnp.swapaxes(ik_t[:, :IDX_DIM, :], 1, 2))
            pb_s.append(_state_from_blockdiag(sbd))
            pb_c.append(tail[:, GDN_TAIL - (CONV_WIDTH - 1):])
            aq, ak, iq, ik, av, misc, conv_in, bz = _proj(
                xs, w_even, _even_groups(F32, F32, 0), gain=g_mix[li], rope=rope_even_s, colscale=cscale,
                tm=tm_s, name="even_in_s")
            a_out = _dsa_sample(iq, ik, misc, aq, ak, av, cache_a_k, cache_a_v, cache_a_idx,
                                page_table, e, Bd=Bd, T=T)
            tail0 = jnp.pad(state_b_conv[e], ((0, 0), (GDN_TAIL - (CONV_WIDTH - 1), 0), (0, 0)))
            b_out, tail, sbd = _gdn(conv_in, bz, misc, tail0, _state_to_blockdiag(state_b_ssm[e]),
                                    *gdn_w, B=Bd, L=T)
            (xs,) = _proj(jnp.concatenate([a_out, b_out], axis=1), w_out, res, residual=xs, tm=tm_s,
                          name="even_out_s")
            sa_k.append(ak.reshape(Bd, T, A_KV_HEADS, A_HEAD_DIM))
            sa_v.append(av.reshape(Bd, T, A_KV_HEADS, A_HEAD_DIM))
            sa_i.append(ik[:, :IDX_DIM].reshape(Bd, T, IDX_DIM))
            sb_s.append(_state_from_blockdiag(sbd).astype(state_b_ssm.dtype))
            sb_c.append(tail[:, GDN_TAIL - (CONV_WIDTH - 1):])
        else:
            o = li // 2
            wd, wq, wkv = _mla_weights(w_down_odd[o], w_uq[o], w_ukv[o])
            w_out = w_out_odd[o].astype(BF16)
            q, kv, lat, kr = _mla_project(xp, g_mix[li], wd, wq, wkv, g_q_lora[o], g_kv_lora[o],
                                          rope_odd_p, tm_p, BF16)
            (xp,) = _proj(_mla_prompt(q, kv, kr, B=Bp, S=S), w_out, res, residual=xp, tm=tm_p,
                          name="odd_out")
            pc_l.append(lat.reshape(Bp, S, C_KV_LORA))
            pc_r.append(kr[:, C_NOPE:C_NOPE + C_ROPE].reshape(Bp, S, C_ROPE))
            q, kv, lat, kr = _mla_project(xs, g_mix[li], wd, wq, wkv, g_q_lora[o], g_kv_lora[o],
                                          rope_odd_s, tm_s, F32)
            a_out = _mla_sample(q, lat, kr, w_ukv[o], cache_c_latent, cache_c_krope, page_table, o,
                                Bd=Bd, T=T)
            (xs,) = _proj(a_out, w_out, res, residual=xs, tm=tm_s, name="odd_out_s")
            sc_l.append(lat.reshape(Bd, T, C_KV_LORA))
            sc_r.append(kr[:, C_NOPE:C_NOPE + C_ROPE].reshape(Bd, T, C_ROPE))
        w_kv = jnp.concatenate([w_xk[li], w_xv[li]], axis=1).astype(BF16)
        mk, mv = _proj(mem, w_kv, ((X_WIDTH, False, F32), (X_WIDTH, False, F32)), gain=g_mem[li],
                       tm=min(256, mem.shape[0]), name="mem_kv")
        pm_k.append(mk.reshape(Bp, N_MEM, X_HEADS, X_HEAD_DIM))
        pm_v.append(mv.reshape(Bp, N_MEM, X_HEADS, X_HEAD_DIM))
        wxq, wxo = w_xq[li].astype(BF16), w_xo[li].astype(BF16)
        xp = _xattn(xp, g_cross[li], wxq, mk.reshape(Bp, N_MEM, X_WIDTH), mv.reshape(Bp, N_MEM, X_WIDTH),
                    wxo, rows_per_batch=S, tm=tm_p)
        xs = _xattn_rows(xs, g_cross[li], wxq, cache_mem_k[li], cache_mem_v[li], wxo, T=T,
                         G=math.gcd(Bd, XATTN_ROWS_PER_STEP))
        w1, w2 = w_ff1[li].astype(BF16), w_ff2[li].astype(BF16)
        last = li == depth - 1
        xp = _mlp(xp, g_mlp[li], w1, w2, g_final, final_norm=last)
        xs = _mlp(xs, g_mlp[li], w1, w2, g_final, final_norm=last)
    return (xp.reshape(Bp, S, D), xs.reshape(Bd, T, D),
            jnp.stack(pa_k, axis=1), jnp.stack(pa_v, axis=1), jnp.stack(pa_i, axis=1),
            jnp.stack(pb_s, axis=0), jnp.stack(pb_c, axis=0),
            jnp.stack(pc_l, axis=1), jnp.stack(pc_r, axis=1),
            jnp.stack(pm_k, axis=0), jnp.stack(pm_v, axis=0),
            jnp.stack(sa_k, axis=1), jnp.stack(sa_v, axis=1), jnp.stack(sa_i, axis=1),
            jnp.stack(sb_s, axis=0), jnp.stack(sb_c, axis=0),
            jnp.stack(sc_l, axis=1), jnp.stack(sc_r, axis=1))
```

```python
import functools
import math

import jax
import jax.numpy as jnp
from jax import lax
from jax.experimental import pallas as pl
from jax.experimental.pallas import tpu as pltpu

F32 = jnp.float32
BF16 = jnp.bfloat16
I32 = jnp.int32

LANES = 128
SUBLANES = 8
VMEM_LIMIT = 56 * 1024 * 1024

EPS = 1e-6
ROPE_THETA = 10000.0
N_MEM = 256
TOPK_MAX = 256

A_HEADS = 8
A_KV_HEADS = 2
A_HEAD_DIM = 64
IDX_HEADS = 8
IDX_DIM = 64
IDX_W_SCALE = (IDX_HEADS * IDX_DIM) ** -0.5
A_SCALE = A_HEAD_DIM ** -0.5

B_HEADS = 8
B_HEAD_DIM = 64
CONV_WIDTH = 4
GDN_CHUNK = 64
GDN_QK = B_HEADS * B_HEAD_DIM
GDN_CONV_DIM = 3 * GDN_QK

C_HEADS = 16
C_NOPE = 64
C_ROPE = 32
C_V = 64
C_Q_LORA = 384
C_KV_LORA = 256
MLA_SCALE = (C_NOPE + C_ROPE) ** -0.5
LOG2_E = math.log2(math.e)

X_HEADS = 4
X_HEAD_DIM = 128
X_WIDTH = X_HEADS * X_HEAD_DIM
X_SCALE = X_HEAD_DIM ** -0.5

INT_MIN = -2 ** 31
NEG_BIG = -1e30


def _params(*sem):
    return pltpu.CompilerParams(dimension_semantics=sem, vmem_limit_bytes=VMEM_LIMIT)


def _rms(x, g):
    return x * lax.rsqrt(jnp.mean(x * x, axis=-1, keepdims=True) + EPS) * g


def _dot(a, b):
    return jnp.dot(a, b, preferred_element_type=F32)


def _dot_nt(a, b):
    return lax.dot_general(a, b, (((1,), (1,)), ((), ())), preferred_element_type=F32)


def _split3(a):
    a1 = a.astype(BF16)
    r = a - a1.astype(F32)
    a2 = r.astype(BF16)
    a3 = (r - a2.astype(F32)).astype(BF16)
    return a1, a2, a3


def _dot_hi(a, b, nt=False):
    f = _dot_nt if nt else _dot
    a1, a2, a3 = _split3(a)
    b1, b2, b3 = _split3(b)
    small = f(a1, b3) + f(a3, b1) + f(a2, b2)
    mid = f(a1, b2) + f(a2, b1)
    return f(a1, b1) + (mid + small)


def _dot_exact_rhs(a, b_bf16):
    a1, a2, a3 = _split3(a)
    return _dot(a1, b_bf16) + (_dot(a2, b_bf16) + _dot(a3, b_bf16))


def _proj_kernel(*refs, groups, has_gain, has_rope, has_scale, has_res, emit_norm, shift):
    it = iter(refs)
    x_ref = next(it)
    g_ref = next(it) if has_gain else None
    w_ref = next(it)
    if has_rope:
        c_ref, sa_ref, sb_ref = next(it), next(it), next(it)
    s_ref = next(it) if has_scale else None
    r_ref = next(it) if has_res else None
    outs = list(it)
    x = x_ref[...].astype(F32)
    if has_gain:
        x = _rms(x, g_ref[...])
    if emit_norm:
        outs[-1][...] = x
    xb = x.astype(BF16)
    col = 0
    plain = iter(outs[:sum(1 for g in groups if g[2] is not None)])
    flipped = iter(outs[sum(1 for g in groups if g[2] is not None):])
    for width, rope, dtype, t_rows in groups:
        y = _dot(xb, w_ref[:, col:col + width])
        if has_scale:
            y = y * s_ref[:, col:col + width]
        if rope:
            n = width // LANES
            c = jnp.concatenate([c_ref[...]] * n, axis=1)
            sa = jnp.concatenate([sa_ref[...]] * n, axis=1)
            sb = jnp.concatenate([sb_ref[...]] * n, axis=1)
            y = y * c + pltpu.roll(y, shift, 1) * sa + pltpu.roll(y, width - shift, 1) * sb
        if has_res:
            y = y + r_ref[...]
        if dtype is not None:
            next(plain)[...] = y.astype(dtype)
        if t_rows:
            next(flipped)[0] = y.T
        col += width


def _proj(x, w, groups, *, gain=None, rope=None, colscale=None, residual=None,
          emit_norm=False, tm=256, name="proj"):
    M, K = x.shape
    N = w.shape[1]
    groups = tuple(tuple(g) + (0,) * (4 - len(g)) for g in groups)
    assert sum(g[0] for g in groups) == N and M % tm == 0
    args, specs = [x], [pl.BlockSpec((tm, K), lambda i: (i, 0))]
    if gain is not None:
        args.append(gain.reshape(1, K).astype(F32))
        specs.append(pl.BlockSpec((1, K), lambda i: (0, 0)))
    args.append(w)
    specs.append(pl.BlockSpec((K, N), lambda i: (0, 0)))
    shift = 0
    if rope is not None:
        c, sa, sb, shift, nblk = rope
        for t in (c, sa, sb):
            args.append(t)
            specs.append(pl.BlockSpec((tm, LANES), lambda i, nblk=nblk: (i % nblk, 0)))
    if colscale is not None:
        args.append(colscale.reshape(1, N).astype(F32))
        specs.append(pl.BlockSpec((1, N), lambda i: (0, 0)))
    if residual is not None:
        assert len(groups) == 1
        args.append(residual)
        specs.append(pl.BlockSpec((tm, N), lambda i: (i, 0)))
    out_shape = [jax.ShapeDtypeStruct((M, g[0]), g[2]) for g in groups if g[2] is not None]
    out_specs = [pl.BlockSpec((tm, g[0]), lambda i: (i, 0)) for g in groups if g[2] is not None]
    for width, _, _, t_rows in groups:
        if t_rows:
            per = t_rows // tm
            assert t_rows % tm == 0
            out_shape.append(jax.ShapeDtypeStruct((M // t_rows, width, t_rows), F32))
            out_specs.append(pl.BlockSpec((1, width, tm), lambda i, per=per: (i // per, 0, i % per)))
    if emit_norm:
        out_shape.append(jax.ShapeDtypeStruct((M, K), F32))
        out_specs.append(pl.BlockSpec((tm, K), lambda i: (i, 0)))
    kern = functools.partial(
        _proj_kernel, groups=groups, has_gain=gain is not None,
        has_rope=rope is not None, has_scale=colscale is not None,
        has_res=residual is not None, emit_norm=emit_norm, shift=shift)
    return pl.pallas_call(
        kern, grid=(M // tm,), in_specs=specs, out_specs=out_specs,
        out_shape=out_shape, compiler_params=_params("parallel"), name=name)(*args)


def _mlp_kernel(x_ref, g_ref, w1_ref, w2_ref, gf_ref, o_ref, hn_ref, acc_ref, *, final_norm):
    j = pl.program_id(1)

    @pl.when(j == 0)
    def _():
        hn_ref[...] = _rms(x_ref[...], g_ref[...]).astype(BF16)
        acc_ref[...] = jnp.zeros_like(acc_ref)

    a = _dot(hn_ref[...], w1_ref[...])
    a = jnp.square(jnp.maximum(a, 0.0)).astype(BF16)
    acc_ref[...] += _dot(a, w2_ref[...])

    @pl.when(j == pl.num_programs(1) - 1)
    def _():
        y = x_ref[...] + acc_ref[...]
        if final_norm:
            y = _rms(y, gf_ref[...])
        o_ref[...] = y


def _mlp(x, g, w1, w2, g_final, *, final_norm, tm=512, tf=1024):
    M, D = x.shape
    F = w1.shape[1]
    tm = min(tm, M)
    assert M % tm == 0 and F % tf == 0
    return pl.pallas_call(
        functools.partial(_mlp_kernel, final_norm=final_norm),
        grid=(M // tm, F // tf),
        in_specs=[pl.BlockSpec((tm, D), lambda i, j: (i, 0)),
                  pl.BlockSpec((1, D), lambda i, j: (0, 0)),
                  pl.BlockSpec((D, tf), lambda i, j: (0, j)),
                  pl.BlockSpec((tf, D), lambda i, j: (j, 0)),
                  pl.BlockSpec((1, D), lambda i, j: (0, 0))],
        out_specs=pl.BlockSpec((tm, D), lambda i, j: (i, 0)),
        out_shape=jax.ShapeDtypeStruct((M, D), F32),
        scratch_shapes=[pltpu.VMEM((tm, D), BF16), pltpu.VMEM((tm, D), F32)],
        compiler_params=_params("parallel", "arbitrary"), name="mlp",
    )(x, g.reshape(1, D), w1, w2, g_final.reshape(1, D))


def _xattn_kernel(x_ref, g_ref, wq_ref, mk_ref, mv_ref, wo_ref, o_ref):
    x = x_ref[...]
    q = _dot(_rms(x, g_ref[...]).astype(BF16), wq_ref[...])
    mk = mk_ref[0].astype(BF16)
    mv = mv_ref[0].astype(BF16)
    heads = []
    for h in range(X_HEADS):
        sl = slice(h * X_HEAD_DIM, (h + 1) * X_HEAD_DIM)
        s = _dot_nt(q[:, sl].astype(BF16), mk[:, sl]) * X_SCALE
        s = s - jnp.max(s, axis=-1, keepdims=True)
        p = jnp.exp(s)
        p = p / jnp.sum(p, axis=-1, keepdims=True)
        heads.append(_dot(p.astype(BF16), mv[:, sl]))
    o = jnp.concatenate(heads, axis=1).astype(BF16)
    o_ref[...] = x + _dot(o, wo_ref[...])


def _xattn(x, g, wq, mk, mv, wo, *, rows_per_batch, tm):
    M, D = x.shape
    assert rows_per_batch % tm == 0
    per = rows_per_batch // tm
    nm = mk.shape[1]
    return pl.pallas_call(
        _xattn_kernel, grid=(M // tm,),
        in_specs=[pl.BlockSpec((tm, D), lambda i: (i, 0)),
                  pl.BlockSpec((1, D), lambda i: (0, 0)),
                  pl.BlockSpec((D, X_WIDTH), lambda i: (0, 0)),
                  pl.BlockSpec((1, nm, X_WIDTH), lambda i: (i // per, 0, 0)),
                  pl.BlockSpec((1, nm, X_WIDTH), lambda i: (i // per, 0, 0)),
                  pl.BlockSpec((X_WIDTH, D), lambda i: (0, 0))],
        out_specs=pl.BlockSpec((tm, D), lambda i: (i, 0)),
        out_shape=jax.ShapeDtypeStruct((M, D), F32),
        compiler_params=_params("parallel"), name="xattn",
    )(x, g.reshape(1, D), wq, mk, mv, wo)


def _xattn_rows_kernel(x_ref, g_ref, wq_ref, mk_ref, mv_ref, wo_ref, o_ref, *, G, T):
    x = x_ref[...]
    q = _dot(_rms(x, g_ref[...]).astype(BF16), wq_ref[...])
    rows = mk_ref.shape[2]
    own = (lax.broadcasted_iota(I32, (X_HEADS * T, rows), 1) % X_HEADS
           == lax.broadcasted_iota(I32, (X_HEADS * T, rows), 0) // T)
    outs = []
    for g in range(G):
        qs = _stack_heads(q[g * T:(g + 1) * T], X_HEADS, X_HEAD_DIM).astype(BF16)
        s = jnp.where(own, _dot_nt(qs, mk_ref[0, g].astype(BF16)) * X_SCALE, -jnp.inf)
        p = jnp.exp(s - jnp.max(s, axis=-1, keepdims=True))
        p = p / jnp.sum(p, axis=-1, keepdims=True)
        o = _dot(p.astype(BF16), mv_ref[0, g].astype(BF16))
        outs.append(jnp.concatenate([o[h * T:(h + 1) * T] for h in range(X_HEADS)], axis=1))
    o_all = jnp.concatenate(outs, axis=0).astype(BF16)
    o_ref[...] = x + _dot(o_all, wo_ref[...])


def _xattn_rows(x, g, wq, mk, mv, wo, layer, *, T, G):
    M, D = x.shape
    B = M // T
    assert B % G == 0
    rows = mk.shape[2] * X_HEADS
    mk = mk.reshape(mk.shape[0], B, rows, X_HEAD_DIM)
    mv = mv.reshape(mv.shape[0], B, rows, X_HEAD_DIM)
    return pl.pallas_call(
        functools.partial(_xattn_rows_kernel, G=G, T=T), grid=(B // G,),
        in_specs=[pl.BlockSpec((G * T, D), lambda i: (i, 0)),
                  pl.BlockSpec((1, D), lambda i: (0, 0)),
                  pl.BlockSpec((D, X_WIDTH), lambda i: (0, 0)),
                  pl.BlockSpec((1, G, rows, X_HEAD_DIM), lambda i: (layer, i, 0, 0)),
                  pl.BlockSpec((1, G, rows, X_HEAD_DIM), lambda i: (layer, i, 0, 0)),
                  pl.BlockSpec((X_WIDTH, D), lambda i: (0, 0))],
        out_specs=pl.BlockSpec((G * T, D), lambda i: (i, 0)),
        out_shape=jax.ShapeDtypeStruct((M, D), F32),
        compiler_params=_params("parallel"), name="xattn_rows",
    )(x, g.reshape(1, D), wq, mk, mv, wo)


def _rope_tables(pos, dim, lane_lo, lane_hi):
    half = dim // 2
    lane = jnp.arange(LANES)
    j = (lane - lane_lo) % dim
    inside = (lane >= lane_lo) & (lane < lane_hi)
    inv = jnp.exp(-math.log(ROPE_THETA) * (j % half).astype(F32) / half)
    ang = pos.astype(F32)[:, None] * inv[None, :]
    cos, sin = jnp.cos(ang), jnp.sin(ang)
    c = jnp.where(inside[None, :], cos, 1.0)
    sa = jnp.where((inside & (j >= half))[None, :], sin, 0.0)
    sb = jnp.where((inside & (j < half))[None, :], -sin, 0.0)
    return c, sa, sb, half


def _mla_prompt_block(q_ref, kv_ref, kr_ref, o_ref, n, tq):
    lo = n * tq
    row = lax.broadcasted_iota(I32, (tq, tq), 0)
    col = lax.broadcasted_iota(I32, (tq, tq), 1)

    def keys(a, b, h):
        kv = kv_ref[a:b, h * LANES:(h + 1) * LANES]
        lane = lax.broadcasted_iota(I32, (b - a, LANES), 1)
        return jnp.where(lane < C_NOPE, kv, kr_ref[a:b, :].astype(kv.dtype)), kv

    outs = []
    for h in range(2):
        q = q_ref[:, h * LANES:(h + 1) * LANES]
        kd, vd = keys(lo, lo + tq, h)
        sd = jnp.where(col <= row, _dot_nt(q, kd), -jnp.inf)
        m = jnp.max(sd, axis=-1, keepdims=True)
        if n > 0:
            ka, va = keys(0, lo, h)
            sa = _dot_nt(q, ka)
            m = jnp.maximum(m, jnp.max(sa, axis=-1, keepdims=True))
        pd = jnp.exp2((sd - m) * (MLA_SCALE * LOG2_E))
        l = jnp.sum(pd, axis=-1, keepdims=True)
        acc = _dot(pd.astype(BF16), vd)
        if n > 0:
            pa = jnp.exp2((sa - m) * (MLA_SCALE * LOG2_E))
            l = l + jnp.sum(pa, axis=-1, keepdims=True)
            acc = acc + _dot(pa.astype(BF16), va)
        outs.append(acc / l)
    lane_o = lax.broadcasted_iota(I32, (tq, LANES), 1)
    o_ref[...] = jnp.where(lane_o < C_V, pltpu.roll(outs[0], C_V, 1), outs[1]).astype(o_ref.dtype)


def _mla_prompt_kernel(q_ref, kv_ref, kr_ref, o_ref, *, tq, nq):
    i = pl.program_id(2)
    for n in range(nq):
        pl.when(i == n)(functools.partial(_mla_prompt_block, q_ref, kv_ref, kr_ref, o_ref, n, tq))


def _mla_prompt(q, kv, kr, *, B, S, tq=512):
    tq = min(tq, S)
    nq = S // tq
    return pl.pallas_call(
        functools.partial(_mla_prompt_kernel, tq=tq, nq=nq),
        grid=(B, C_HEADS // 2, nq),
        in_specs=[pl.BlockSpec((tq, 2 * LANES), lambda b, h, i: (b * nq + i, h)),
                  pl.BlockSpec((S, 2 * LANES), lambda b, h, i: (b, h)),
                  pl.BlockSpec((S, LANES), lambda b, h, i: (b, 0))],
        out_specs=pl.BlockSpec((tq, LANES), lambda b, h, i: (b * nq + i, h)),
        out_shape=jax.ShapeDtypeStruct((B * S, C_HEADS * C_V), BF16),
        compiler_params=_params("parallel", "parallel", "arbitrary"), name="mla_prompt",
    )(q, kv, kr)


def _order_key(sc):
    sc = jnp.where(sc == 0.0, 0.0, sc)
    bits = lax.bitcast_convert_type(sc, I32)
    return bits ^ ((bits >> 31) & 0x7FFFFFFF)


def _topk_search(count, k, n_idx_bits):
    kf = float(k)
    t0 = jnp.where(count(lambda key, idx: key >= 0) >= kf, 0, INT_MIN).astype(I32)

    def vbody(n, t):
        cand = t | jnp.left_shift(jnp.int32(1), 30 - n)
        return jnp.where(count(lambda key, idx: key >= cand) >= kf, cand, t)

    t = lax.fori_loop(0, 31, vbody, t0)
    need = kf - count(lambda key, idx: key > t)
    n_eq = count(lambda key, idx: key == t)

    def search():
        def ibody(n, x):
            cand = x | jnp.left_shift(jnp.int32(1), n_idx_bits - 1 - n)
            return jnp.where(count(lambda key, idx: (key == t) & (idx < cand)) < need, cand, x)

        return lax.fori_loop(0, n_idx_bits, ibody, jnp.zeros_like(t))

    tied = jnp.max(jnp.where((n_eq > need) & (t > INT_MIN), 1.0, 0.0)) > 0.5
    x = lax.cond(tied, search, lambda: jnp.full_like(t, (1 << n_idx_bits) - 1))
    return t, x


def _topk_mask(key, idx, k, axis, n_idx_bits):
    def count(pred):
        return jnp.sum(pred(key, idx).astype(F32), axis=axis, keepdims=True)

    t, x = _topk_search(count, k, n_idx_bits)
    return (key > t) | ((key == t) & (idx <= x))


def _dsa_prompt_kernel(iq_ref, ik_ref, iw_ref, aq_ref, ak_ref, avt_ref, o_ref,
                       key_ref, m_ref, l_ref, acc_ref, *, tq, topk):
    i = pl.program_id(1)
    S = ik_ref.shape[0]
    n_chunks = i + 1
    kofs = lax.broadcasted_iota(I32, (tq, tq), 0)
    qidx = i * tq + lax.broadcasted_iota(I32, (tq, tq), 1)

    def chunk_start(c):
        return pl.multiple_of(c * tq, tq)

    def score_chunk(c, carry):
        ik = ik_ref[pl.ds(chunk_start(c), tq), :].astype(BF16)
        sc = jnp.zeros((tq, tq), F32)
        for h in range(IDX_HEADS):
            s = _dot_nt(ik, iq_ref[:, h * LANES:(h + 1) * LANES])
            sc = sc + iw_ref[h:h + 1, :] * jnp.maximum(s, 0.0)
        key_ref[pl.ds(chunk_start(c), tq), :] = jnp.where(kofs + c * tq <= qidx, _order_key(sc), INT_MIN)
        return carry

    lax.fori_loop(0, n_chunks, score_chunk, 0)

    def count(pred):
        def body(c, acc):
            hit = pred(key_ref[pl.ds(chunk_start(c), tq), :], kofs + c * tq)
            return acc + jnp.sum(hit.astype(F32).reshape(tq // SUBLANES, SUBLANES, tq), axis=0)

        part = lax.fori_loop(0, n_chunks, body, jnp.zeros((SUBLANES, tq), F32))
        return jnp.sum(part, axis=0, keepdims=True)

    t, x = _topk_search(count, topk, max(1, (S - 1).bit_length()))

    m_ref[...] = jnp.full(m_ref.shape, NEG_BIG, F32)
    l_ref[...] = jnp.zeros(l_ref.shape, F32)
    acc_ref[...] = jnp.zeros(acc_ref.shape, F32)
    rep = A_HEADS // A_KV_HEADS

    def attend_chunk(c, carry):
        start = chunk_start(c)
        key = key_ref[pl.ds(start, tq), :]
        kidx = kofs + c * tq
        sel = ((key > t) | ((key == t) & (kidx <= x))) & (kidx <= qidx)
        ak = ak_ref[pl.ds(start, tq), :].astype(BF16)
        avt = avt_ref[0, :, pl.ds(start, tq)].astype(BF16)
        heads = range(A_HEADS)
        s = [jnp.where(sel, _dot_nt(ak, aq_ref[:, h * LANES:(h + 1) * LANES]), NEG_BIG) for h in heads]
        m_old = m_ref[...]
        m_new = jnp.maximum(m_old, jnp.concatenate([jnp.max(s[h], axis=0, keepdims=True) for h in heads],
                                                   axis=0))
        corr = jnp.exp(m_old - m_new)
        p = [jnp.exp(s[h] - m_new[h:h + 1, :]) for h in heads]
        l_ref[...] = l_ref[...] * corr + jnp.concatenate(
            [jnp.sum(p[h], axis=0, keepdims=True) for h in heads], axis=0)
        pv = [_dot(avt[(h // rep) * A_HEAD_DIM:(h // rep + 1) * A_HEAD_DIM], p[h].astype(BF16))
              for h in heads]
        for h in heads:
            acc_ref[h] = acc_ref[h] * corr[h:h + 1, :] + pv[h]
        m_ref[...] = m_new
        return carry

    lax.fori_loop(0, n_chunks, attend_chunk, 0)
    outs = [acc_ref[h] / l_ref[h:h + 1, :] for h in range(A_HEADS)]
    o_ref[...] = jnp.concatenate(outs, axis=0).T.astype(o_ref.dtype)


def _dsa_prompt(iq, ik, iw_t, aq, ak, av_t, *, B, S, tq=256):
    tq = min(tq, S)
    nq = S // tq
    topk = min(TOPK_MAX, S // 4)
    W = A_HEADS * LANES
    return pl.pallas_call(
        functools.partial(_dsa_prompt_kernel, tq=tq, topk=topk),
        grid=(B, nq),
        scratch_shapes=[pltpu.VMEM((S, tq), I32), pltpu.VMEM((A_HEADS, tq), F32),
                        pltpu.VMEM((A_HEADS, tq), F32), pltpu.VMEM((A_HEADS, A_HEAD_DIM, tq), F32)],
        in_specs=[pl.BlockSpec((tq, W), lambda b, i: (b * nq + i, 0)),
                  pl.BlockSpec((S, LANES), lambda b, i: (b, 0)),
                  pl.BlockSpec((IDX_HEADS, tq), lambda b, i: (0, b * nq + i)),
                  pl.BlockSpec((tq, W), lambda b, i: (b * nq + i, 0)),
                  pl.BlockSpec((S, LANES), lambda b, i: (b, 0)),
                  pl.BlockSpec((1, LANES, S), lambda b, i: (b, 0, 0))],
        out_specs=pl.BlockSpec((tq, A_HEADS * A_HEAD_DIM), lambda b, i: (b * nq + i, 0)),
        out_shape=jax.ShapeDtypeStruct((B * S, A_HEADS * A_HEAD_DIM), BF16),
        compiler_params=_params("parallel", "arbitrary"), name="dsa_prompt",
    )(iq, ik, iw_t, aq, ak, av_t)


GDN_BETA_LANE = 8
GDN_DECAY_LANE = 16
GDN_TAIL = SUBLANES


def _softplus(x):
    return jnp.maximum(x, 0.0) + jnp.log1p(jnp.exp(-jnp.abs(x)))


def _silu(x):
    return x * jax.nn.sigmoid(x)


def _dot3(a, b, nt=False):
    f = _dot_nt if nt else _dot
    a1 = a.astype(BF16)
    a2 = (a - a1.astype(F32)).astype(BF16)
    b1 = b.astype(BF16)
    b2 = (b - b1.astype(F32)).astype(BF16)
    return f(a1, b1) + (f(a1, b2) + f(a2, b1))


def _gdn_kernel(xin_ref, z_ref, misc_ref, misct_ref, tail0_ref, s0_ref, cw_ref, alog_ref, dtb_ref,
                alogt_ref, dtbt_ref, ng_ref, o_ref, tail_ref, sout_ref, xp_ref, s_ref, *, C, NB):
    c = pl.program_id(1)
    NP = B_HEADS // 2

    @pl.when(c == 0)
    def _():
        xp_ref[:, 0:GDN_TAIL, :] = tail0_ref[...]
        s_ref[...] = s0_ref[...]

    @pl.when(c > 0)
    def _():
        xp_ref[:, 0:GDN_TAIL, :] = xp_ref[:, C:C + GDN_TAIL, :]

    xp_ref[:, GDN_TAIL:GDN_TAIL + C, :] = xin_ref[...]
    tail_ref[...] = xp_ref[:, C:C + GDN_TAIL, :]
    base = GDN_TAIL - (CONV_WIDTH - 1)
    ti = lax.broadcasted_iota(I32, (C, C), 0)
    tj = lax.broadcasted_iota(I32, (C, C), 1)
    low = (tj <= ti).astype(BF16)
    upp = (ti <= tj).astype(BF16)

    first = lax.broadcasted_iota(I32, (C, LANES), 1) < B_HEAD_DIM
    li = lax.broadcasted_iota(I32, (LANES, LANES), 0)
    lj = lax.broadcasted_iota(I32, (LANES, LANES), 1)
    blockdiag = (li < B_HEAD_DIM) == (lj < B_HEAD_DIM)
    ones_bd = blockdiag.astype(BF16)
    own_lanes = ((lax.broadcasted_iota(I32, (2 * C, LANES), 0) < C)
                 == (lax.broadcasted_iota(I32, (2 * C, LANES), 1) < B_HEAD_DIM))
    ri = lax.broadcasted_iota(I32, (2 * C, 2 * C), 0)
    rj = lax.broadcasted_iota(I32, (2 * C, 2 * C), 1)
    same_head = (ri < C) == (rj < C)
    incl = same_head & (rj <= ri)
    strict = same_head & (rj < ri)
    eye = (ri == rj).astype(F32)

    def pair(col0, col1):
        return jnp.where(first, col0, col1)

    def halves(x):
        return pair(x[:C, :LANES], x[C:, :LANES])

    def twice(x):
        return jnp.concatenate([x, x], axis=0)

    st = []
    for bb, p in [(bb, p) for bb in range(NB) for p in range(NP)]:
        if p == 0:
            y = xp_ref[bb, base:base + C, :] * cw_ref[0:1, :]
            for i in range(1, CONV_WIDTH):
                y = y + xp_ref[bb, base + i:base + i + C, :] * cw_ref[i:i + 1, :]
            xc = _silu(y)
            misc = misc_ref[bb]
            beta = jax.nn.sigmoid(misc)
            g = -jnp.exp(alog_ref[...]) * _softplus(misc + dtb_ref[...])
            gt = -jnp.exp(alogt_ref[...]) * _softplus(misct_ref[bb, 0] + dtbt_ref[...])
            g1, g2, g3 = _split3(g)
            gc = _dot(low, g1) + (_dot(low, g2) + _dot(low, g3))
            t1, t2, t3 = _split3(gt)
            gct = _dot(t1, upp) + (_dot(t2, upp) + _dot(t3, upp))
        sl = slice(p * LANES, (p + 1) * LANES)
        q2 = xc[:, sl]
        k2 = xc[:, GDN_QK + p * LANES:GDN_QK + (p + 1) * LANES]
        v2 = xc[:, 2 * GDN_QK + p * LANES:2 * GDN_QK + (p + 1) * LANES]
        q2 = q2 * lax.rsqrt(_dot_exact_rhs(q2 * q2, ones_bd) + EPS) * (B_HEAD_DIM ** -0.5)
        k2 = k2 * lax.rsqrt(_dot_exact_rhs(k2 * k2, ones_bd) + EPS)
        bcol = [beta[:, GDN_BETA_LANE + 2 * p + e:GDN_BETA_LANE + 2 * p + e + 1] for e in range(2)]
        gcol = [gc[:, GDN_DECAY_LANE + 2 * p + e:GDN_DECAY_LANE + 2 * p + e + 1] for e in range(2)]
        glast = [gcol[e][C - 1:C, :] for e in range(2)]
        beta2 = pair(bcol[0], bcol[1])
        gam2 = jnp.exp(pair(gcol[0], gcol[1]))
        kd2 = k2 * jnp.exp(pair(glast[0] - gcol[0], glast[1] - gcol[1]))
        gl2 = jnp.exp(jnp.where(first[0:1, :], glast[0], glast[1]))
        rhs = twice(jnp.concatenate([v2 * beta2, k2 * (beta2 * gam2)], axis=1))
        grow = jnp.concatenate([gct[GDN_DECAY_LANE + 2 * p + e:GDN_DECAY_LANE + 2 * p + e + 1, :]
                                for e in range(2)], axis=1)
        decay = jnp.exp(jnp.where(incl, jnp.concatenate(gcol, axis=0) - grow, -jnp.inf))
        kx = jnp.where(own_lanes, twice(k2), 0.0)
        qx = jnp.where(own_lanes, twice(q2), 0.0)
        a = jnp.where(strict, jnp.concatenate(bcol, axis=0) * _dot3(kx, kx, nt=True) * decay, 0.0)
        aqk = _dot_nt(qx.astype(BF16), kx.astype(BF16)) * decay
        st.append(dict(bb=bb, p=p, sl=sl, q2=q2, gam2=gam2, kd2=kd2, gl2=gl2, rhs=rhs, aqk=aqk, a=a))

    tinv = [eye - d["a"] for d in st]
    xk = [d["a"] for d in st]
    span = 2
    while span < C:
        xk = [_dot3(x, x) for x in xk]
        tinv = [t + _dot3(t, x) for t, x in zip(tinv, xk)]
        span *= 2
    sols = [_dot3(t, d["rhs"]) for t, d in zip(tinv, st)]

    for d, sol in zip(st, sols):
        bb, p = d["bb"], d["p"]
        u2 = halves(sol[:, :LANES])
        w2 = halves(sol[:, LANES:])
        s_old = s_ref[bb, p]
        sb = s_old.astype(BF16)
        delta = u2 - _dot(w2.astype(BF16), sb)
        db = delta.astype(BF16)
        o2 = _dot((d["q2"] * d["gam2"]).astype(BF16), sb) + halves(_dot(d["aqk"].astype(BF16), twice(db)))
        upd = lax.dot_general(d["kd2"].astype(BF16), db, (((0,), (0,)), ((), ())),
                              preferred_element_type=F32)
        s_new = s_old * d["gl2"] + jnp.where(blockdiag, upd, 0.0)
        s_ref[bb, p] = s_new
        sout_ref[bb, p] = s_new
        ms = _dot_exact_rhs(o2 * o2, ones_bd) * (1.0 / B_HEAD_DIM)
        o_ref[bb, :, d["sl"]] = (o2 * lax.rsqrt(ms + EPS) * ng_ref[...]
                                 * _silu(z_ref[bb, :, d["sl"]])).astype(o_ref.dtype)


GDN_BATCH_PER_STEP = 2


def _gdn(xin, z, misc, tail0, s0, conv_w, a_log, dt_bias, norm_g, *, B, L):
    C = math.gcd(L, GDN_CHUNK)
    n = L // C
    M = B * L
    NP = B_HEADS // 2
    NB = math.gcd(B, GDN_BATCH_PER_STEP)
    nrow = GDN_DECAY_LANE + B_HEADS
    misc_t = jnp.swapaxes(misc[:, :nrow].reshape(B, n, C, nrow), 2, 3)
    xin, z, misc = (a.reshape(B, L, a.shape[1]) for a in (xin, z, misc))
    lane_vec = lambda v: jnp.zeros((1, LANES), F32).at[0, GDN_DECAY_LANE:nrow].set(v)
    col_vec = lambda v: jnp.zeros((nrow, 1), F32).at[GDN_DECAY_LANE:, 0].set(v)
    cw = jnp.pad(conv_w, ((0, SUBLANES - CONV_WIDTH), (0, 0)))
    ng = jnp.tile(norm_g.reshape(1, B_HEAD_DIM), (1, 2))
    full = lambda shape: pl.BlockSpec(shape, lambda b, c: (0,) * len(shape))
    o, tail, state = pl.pallas_call(
        functools.partial(_gdn_kernel, C=C, NB=NB),
        grid=(B // NB, n),
        in_specs=[pl.BlockSpec((NB, C, GDN_CONV_DIM), lambda b, c: (b, c, 0)),
                  pl.BlockSpec((NB, C, GDN_QK), lambda b, c: (b, c, 0)),
                  pl.BlockSpec((NB, C, LANES), lambda b, c: (b, c, 0)),
                  pl.BlockSpec((NB, 1, nrow, C), lambda b, c: (b, c, 0, 0)),
                  pl.BlockSpec((NB, GDN_TAIL, GDN_CONV_DIM), lambda b, c: (b, 0, 0)),
                  pl.BlockSpec((NB, NP, LANES, LANES), lambda b, c: (b, 0, 0, 0)),
                  full((SUBLANES, GDN_CONV_DIM)), full((1, LANES)), full((1, LANES)),
                  full((nrow, 1)), full((nrow, 1)), full((1, LANES))],
        out_specs=[pl.BlockSpec((NB, C, GDN_QK), lambda b, c: (b, c, 0)),
                   pl.BlockSpec((NB, GDN_TAIL, GDN_CONV_DIM), lambda b, c: (b, 0, 0)),
                   pl.BlockSpec((NB, NP, LANES, LANES), lambda b, c: (b, 0, 0, 0))],
        out_shape=[jax.ShapeDtypeStruct((B, L, GDN_QK), BF16),
                   jax.ShapeDtypeStruct((B, GDN_TAIL, GDN_CONV_DIM), F32),
                   jax.ShapeDtypeStruct((B, NP, LANES, LANES), F32)],
        scratch_shapes=[pltpu.VMEM((NB, C + GDN_TAIL, GDN_CONV_DIM), F32),
                        pltpu.VMEM((NB, NP, LANES, LANES), F32)],
        compiler_params=_params("arbitrary", "arbitrary"), name="gdn",
    )(xin, z, misc, misc_t, tail0, s0, cw, lane_vec(a_log), lane_vec(dt_bias),
      col_vec(a_log), col_vec(dt_bias), ng)
    return o.reshape(M, GDN_QK), tail, state


def _state_to_blockdiag(s):
    B = s.shape[0]
    d = B_HEAD_DIM
    s = s.reshape(B, B_HEADS // 2, 2, d, d)
    z = jnp.zeros_like(s[:, :, 0])
    top = jnp.concatenate([s[:, :, 0], z], axis=-1)
    bot = jnp.concatenate([z, s[:, :, 1]], axis=-1)
    return jnp.concatenate([top, bot], axis=-2)


def _state_from_blockdiag(s):
    d = B_HEAD_DIM
    return jnp.stack([s[:, :, :d, :d], s[:, :, d:, d:]], axis=2).reshape(s.shape[0], B_HEADS, d, d)


PAGES_PER_STEP = 32
SCORE_PAGES_PER_STEP = 32
XATTN_ROWS_PER_STEP = 8
PAGES_PER_BLOCK = 8
SELECT_ROWS = 64


def _page_specs(block, layer, n):
    nd = len(block)

    def spec(p):
        return pl.BlockSpec(block, lambda b, j, pt, p=p: (pt[b, j * n + p], layer) + (0,) * (nd - 2))

    return [spec(p) for p in range(n)]


def _stack_heads(x, n_heads, width=LANES):
    return jnp.concatenate([x[:, h * width:(h + 1) * width] for h in range(n_heads)], axis=0)


def _pad_rows(x, rows):
    return jnp.concatenate([x, jnp.zeros((rows - x.shape[0], x.shape[1]), x.dtype)], axis=0)


def _dsa_s_score_kernel(pt_ref, iq_ref, misc_ref, *refs, n_pg, T):
    pages, o_ref = refs[:n_pg], refs[n_pg]
    q = _stack_heads(iq_ref[...], IDX_HEADS)[:, :IDX_DIM].astype(BF16)
    misc = misc_ref[...]
    w = jnp.concatenate([misc[:, h:h + 1] for h in range(IDX_HEADS)], axis=0)
    keys_t = jnp.concatenate([pages[p][0, 0].astype(BF16) for p in range(n_pg)], axis=1)
    r = jnp.maximum(_dot(q, keys_t), 0.0) * w
    sc = r[0:T]
    for h in range(1, IDX_HEADS):
        sc = sc + r[h * T:(h + 1) * T]
    o_ref[0] = sc


def _dsa_s_select_kernel(sp_ref, iq_ref, ikn_ref, misc_ref, bp_ref, bn_ref, *, G, T, topk):
    past = sp_ref.shape[2]
    R = G * T
    misc = misc_ref[...]
    iq = iq_ref[...]
    ikn = _pad_rows(ikn_ref[...], LANES).astype(BF16)
    sn = jnp.zeros((R, LANES), F32)
    for h in range(IDX_HEADS):
        s = _dot_nt(iq[:, h * LANES:(h + 1) * LANES].astype(BF16), ikn)
        sn = sn + misc[:, h:h + 1] * jnp.maximum(s, 0.0)
    row = lax.broadcasted_iota(I32, (R, LANES), 0)
    lane = lax.broadcasted_iota(I32, (R, LANES), 1)
    visible = (lane // T == row // T) & (lane % T <= row % T)
    key_new = jnp.where(visible, _order_key(sn), INT_MIN)
    key = jnp.concatenate([_order_key(sp_ref[...].reshape(R, past)), key_new], axis=1)
    idx = lax.broadcasted_iota(I32, key.shape, 1)
    sel = _topk_mask(key, idx, topk, 1, (past + LANES - 1).bit_length())
    bias = jnp.where(sel, 0.0, NEG_BIG)
    bp_ref[...] = bias[:, :past].reshape(G, T, past)
    bn_ref[...] = jnp.where(visible, bias[:, past:], NEG_BIG).reshape(G, T, LANES)


def _online_softmax_step(s, v, m_ref, l_ref, acc_ref, v_transposed=False):
    m_old = m_ref[...]
    m_new = jnp.maximum(m_old, jnp.max(s, axis=-1, keepdims=True))
    corr = jnp.exp(m_old - m_new)
    p = jnp.exp(s - m_new)
    l_ref[...] = l_ref[...] * corr + jnp.sum(p, axis=-1, keepdims=True)
    pv = _dot_nt(p.astype(BF16), v) if v_transposed else _dot(p.astype(BF16), v)
    acc_ref[...] = acc_ref[...] * corr + pv
    m_ref[...] = m_new


def _softmax_blocks_step(blocks, m_ref, l_ref, acc_ref, v_transposed=False):
    ms = [jnp.max(s, axis=-1, keepdims=True) for s, _ in blocks]
    ps = [jnp.exp(s - m_b) for (s, _), m_b in zip(blocks, ms)]
    ls = [jnp.sum(p, axis=-1, keepdims=True) for p in ps]
    pvs = [_dot_nt(p.astype(BF16), v) if v_transposed else _dot(p.astype(BF16), v)
           for p, (_, v) in zip(ps, blocks)]
    parts = list(zip(ms, ls, pvs))
    m_old = m_ref[...]
    m_new = m_old
    for m_b, _, _ in parts:
        m_new = jnp.maximum(m_new, m_b)
    corr = jnp.exp(m_old - m_new)
    l = l_ref[...] * corr
    acc = acc_ref[...] * corr
    for m_b, l_b, pv in parts:
        w = jnp.exp(m_b - m_new)
        l = l + l_b * w
        acc = acc + pv * w
    l_ref[...] = l
    acc_ref[...] = acc
    m_ref[...] = m_new


def _dsa_s_attend_kernel(pt_ref, aq_ref, akn_ref, avn_ref, bp_ref, bn_ref, *refs, n_pg, T):
    kpages, vpages = refs[:n_pg], refs[n_pg:2 * n_pg]
    o_ref, q_ref, m_ref, l_ref, acc_ref = refs[2 * n_pg:]
    j = pl.program_id(1)

    @pl.when(j == 0)
    def _():
        q_ref[...] = _stack_heads(aq_ref[...], A_HEADS).astype(BF16)
        m_ref[...] = jnp.full(m_ref.shape, NEG_BIG, F32)
        l_ref[...] = jnp.zeros(l_ref.shape, F32)
        acc_ref[...] = jnp.zeros(acc_ref.shape, F32)
        kn = _pad_rows(akn_ref[...], LANES).astype(BF16)
        vn = _pad_rows(avn_ref[...], LANES).astype(BF16)
        s = _dot_nt(q_ref[...], kn) + jnp.concatenate([bn_ref[0]] * A_HEADS, axis=0)
        _online_softmax_step(s, vn, m_ref, l_ref, acc_ref)

    q = q_ref[...]
    blocks = []
    for p0 in range(0, n_pg, PAGES_PER_BLOCK):
        group = range(p0, min(p0 + PAGES_PER_BLOCK, n_pg))
        k_t = jnp.concatenate([kpages[p][0, 0].astype(BF16) for p in group], axis=1)
        v_t = jnp.concatenate([vpages[p][0, 0].astype(BF16) for p in group], axis=1)
        ps = kpages[0].shape[3]
        bias = bp_ref[0, :, p0 * ps:(group[-1] + 1) * ps]
        blocks.append((_dot(q, k_t) + jnp.concatenate([bias] * A_HEADS, axis=0), v_t))
    _softmax_blocks_step(blocks, m_ref, l_ref, acc_ref, v_transposed=True)

    @pl.when(j == pl.num_programs(1) - 1)
    def _():
        o = acc_ref[...] / l_ref[...]
        lane = lax.broadcasted_iota(I32, (T, LANES), 1)
        rep = A_HEADS // A_KV_HEADS
        for hp in range(A_HEADS // 2):
            a = o[(2 * hp) * T:(2 * hp + 1) * T]
            b = o[(2 * hp + 1) * T:(2 * hp + 2) * T]
            if (2 * hp) // rep == 0:
                b = pltpu.roll(b, A_HEAD_DIM, 1)
            else:
                a = pltpu.roll(a, A_HEAD_DIM, 1)
            o_ref[:, hp * LANES:(hp + 1) * LANES] = jnp.where(lane < A_HEAD_DIM, a, b).astype(o_ref.dtype)


def _dsa_sample(iq, ikn, misc, aq, akn, avn, pool_k, pool_v, pool_idx, page_table, layer, *, Bd, T):
    n_pages = page_table.shape[1]
    page = pool_idx.shape[2]
    past = n_pages * page
    n_pg = math.gcd(PAGES_PER_STEP, n_pages)
    npg = n_pages // n_pg
    topk = min(TOPK_MAX, (past + T) // 4)
    W = A_HEADS * LANES
    kv_dim = A_KV_HEADS * A_HEAD_DIM
    pk = jnp.transpose(pool_k, (0, 1, 3, 4, 2)).reshape(pool_k.shape[:2] + (kv_dim, page))
    pv = jnp.transpose(pool_v, (0, 1, 3, 4, 2)).reshape(pool_v.shape[:2] + (kv_dim, page))
    pi = jnp.swapaxes(pool_idx, 2, 3)
    row = lambda w: pl.BlockSpec((T, w), lambda b, j, pt: (b, 0))

    n_sc = math.gcd(SCORE_PAGES_PER_STEP, n_pages)
    scores = pl.pallas_call(
        functools.partial(_dsa_s_score_kernel, n_pg=n_sc, T=T),
        grid_spec=pltpu.PrefetchScalarGridSpec(
            num_scalar_prefetch=1, grid=(Bd, n_pages // n_sc),
            in_specs=[row(W), row(LANES)] + _page_specs((1, 1, IDX_DIM, page), layer, n_sc),
            out_specs=pl.BlockSpec((1, T, n_sc * page), lambda b, j, pt: (b, 0, j))),
        out_shape=jax.ShapeDtypeStruct((Bd, T, past), F32),
        compiler_params=_params("parallel", "arbitrary"), name="dsa_s_score",
    )(page_table, iq, misc, *([pi] * n_sc))

    G = math.gcd(Bd, max(1, SELECT_ROWS // T))
    assert G * T <= LANES
    bias_p, bias_n = pl.pallas_call(
        functools.partial(_dsa_s_select_kernel, G=G, T=T, topk=topk),
        grid=(Bd // G,),
        in_specs=[pl.BlockSpec((G, T, past), lambda b: (b, 0, 0)),
                  pl.BlockSpec((G * T, W), lambda b: (b, 0)),
                  pl.BlockSpec((G * T, LANES), lambda b: (b, 0)),
                  pl.BlockSpec((G * T, LANES), lambda b: (b, 0))],
        out_specs=[pl.BlockSpec((G, T, past), lambda b: (b, 0, 0)),
                   pl.BlockSpec((G, T, LANES), lambda b: (b, 0, 0))],
        out_shape=[jax.ShapeDtypeStruct((Bd, T, past), F32),
                   jax.ShapeDtypeStruct((Bd, T, LANES), F32)],
        compiler_params=_params("parallel"), name="dsa_s_select",
    )(scores, iq, ikn, misc)

    group_rows = pl.BlockSpec((G * T, LANES), lambda b, j, pt: (b // G, 0))
    return pl.pallas_call(
        functools.partial(_dsa_s_attend_kernel, n_pg=n_pg, T=T),
        grid_spec=pltpu.PrefetchScalarGridSpec(
            num_scalar_prefetch=1, grid=(Bd, npg),
            in_specs=[row(W), group_rows, group_rows,
                      pl.BlockSpec((1, T, n_pg * page), lambda b, j, pt: (b, 0, j)),
                      pl.BlockSpec((1, T, LANES), lambda b, j, pt: (b, 0, 0))]
            + _page_specs((1, 1, kv_dim, page), layer, n_pg) * 2,
            out_specs=pl.BlockSpec((T, A_HEADS * A_HEAD_DIM), lambda b, j, pt: (b, 0)),
            scratch_shapes=[pltpu.VMEM((A_HEADS * T, LANES), BF16),
                            pltpu.VMEM((A_HEADS * T, 1), F32),
                            pltpu.VMEM((A_HEADS * T, 1), F32),
                            pltpu.VMEM((A_HEADS * T, LANES), F32)]),
        out_shape=jax.ShapeDtypeStruct((Bd * T, A_HEADS * A_HEAD_DIM), BF16),
        compiler_params=_params("parallel", "arbitrary"), name="dsa_s_attend",
    )(page_table, aq, akn, avn, bias_p, bias_n, *([pk] * n_pg), *([pv] * n_pg))


def _head_matmul_kernel(x_ref, w_ref, o_ref):
    o_ref[0] = _dot(x_ref[...].astype(BF16), w_ref[0])


def _mla_s_attend_kernel(pt_ref, ql_ref, q_ref, latn_ref, krn_ref, *refs, n_pg, T):
    lpages, rpages = refs[:n_pg], refs[n_pg:2 * n_pg]
    o_ref, qlat_ref, qr_ref, m_ref, l_ref, acc_ref = refs[2 * n_pg:]
    j = pl.program_id(1)
    R = C_HEADS * T
    lo = C_NOPE

    @pl.when(j == 0)
    def _():
        qlat_ref[...] = ql_ref[...].reshape(R, C_KV_LORA).astype(BF16)
        q = q_ref[...]
        qr_ref[...] = jnp.concatenate(
            [q[:, h * LANES + lo:h * LANES + lo + C_ROPE] for h in range(C_HEADS)], axis=0).astype(BF16)
        m_ref[...] = jnp.full(m_ref.shape, NEG_BIG, F32)
        l_ref[...] = jnp.zeros(l_ref.shape, F32)
        acc_ref[...] = jnp.zeros(acc_ref.shape, F32)
        latn = _pad_rows(latn_ref[...], LANES).astype(BF16)
        krn = _pad_rows(krn_ref[...][:, lo:lo + C_ROPE], LANES).astype(BF16)
        s = (_dot_nt(qlat_ref[...], latn) + _dot_nt(qr_ref[...], krn)) * MLA_SCALE
        t_row = lax.broadcasted_iota(I32, (R, LANES), 0) % T
        n_key = lax.broadcasted_iota(I32, (R, LANES), 1)
        s = jnp.where(n_key <= t_row, s, NEG_BIG)
        _online_softmax_step(s, latn, m_ref, l_ref, acc_ref)

    ql = qlat_ref[...]
    qr = qr_ref[...]
    blocks = []
    for p0 in range(0, n_pg, PAGES_PER_BLOCK):
        group = range(p0, min(p0 + PAGES_PER_BLOCK, n_pg))
        lat = jnp.concatenate([lpages[p][0, 0].astype(BF16) for p in group], axis=0)
        kr_t = jnp.concatenate([rpages[p][0, 0].astype(BF16) for p in group], axis=1)
        blocks.append(((_dot_nt(ql, lat) + _dot(qr, kr_t)) * MLA_SCALE, lat))
    _softmax_blocks_step(blocks, m_ref, l_ref, acc_ref)

    @pl.when(j == pl.num_programs(1) - 1)
    def _():
        o_ref[...] = (acc_ref[...] / l_ref[...]).reshape(C_HEADS, T, C_KV_LORA)


def _pair_matmul_kernel(a_ref, b_ref, wa_ref, wb_ref, o_ref):
    o_ref[...] = (_dot(a_ref[0].astype(BF16), wa_ref[0])
                  + _dot(b_ref[0].astype(BF16), wb_ref[0])).astype(o_ref.dtype)


def _mla_sample(q, latn, krn, w_ukv, pool_lat, pool_kr, page_table, layer, *, Bd, T):
    Ms = Bd * T
    n_pages = page_table.shape[1]
    page = pool_lat.shape[2]
    n_pg = math.gcd(PAGES_PER_STEP, n_pages)
    npg = n_pages // n_pg
    w_uk_t = jnp.pad(jnp.transpose(w_ukv[..., :C_NOPE], (1, 2, 0)),
                     ((0, 0), (0, LANES - C_NOPE), (0, 0))).astype(BF16)
    wv = jnp.transpose(w_ukv[..., C_NOPE:], (1, 0, 2))
    even = (jnp.arange(C_HEADS) % 2 == 0)[:, None, None]
    z = jnp.zeros_like(wv)
    w_uv = jnp.concatenate([jnp.where(even, wv, z), jnp.where(even, z, wv)], axis=-1).astype(BF16)

    q_lat = pl.pallas_call(
        _head_matmul_kernel, grid=(C_HEADS,),
        in_specs=[pl.BlockSpec((Ms, LANES), lambda h: (0, h)),
                  pl.BlockSpec((1, LANES, C_KV_LORA), lambda h: (h, 0, 0))],
        out_specs=pl.BlockSpec((1, Ms, C_KV_LORA), lambda h: (h, 0, 0)),
        out_shape=jax.ShapeDtypeStruct((C_HEADS, Ms, C_KV_LORA), F32),
        compiler_params=_params("parallel"), name="mla_s_qlat",
    )(q, w_uk_t)

    R = C_HEADS * T
    row = lambda w: pl.BlockSpec((T, w), lambda b, j, pt: (b, 0))
    o_lat = pl.pallas_call(
        functools.partial(_mla_s_attend_kernel, n_pg=n_pg, T=T),
        grid_spec=pltpu.PrefetchScalarGridSpec(
            num_scalar_prefetch=1, grid=(Bd, npg),
            in_specs=[pl.BlockSpec((C_HEADS, T, C_KV_LORA), lambda b, j, pt: (0, b, 0)),
                      row(C_HEADS * LANES), row(C_KV_LORA), row(LANES)]
            + _page_specs((1, 1, page, C_KV_LORA), layer, n_pg)
            + _page_specs((1, 1, C_ROPE, page), layer, n_pg),
            out_specs=pl.BlockSpec((C_HEADS, T, C_KV_LORA), lambda b, j, pt: (0, b, 0)),
            scratch_shapes=[pltpu.VMEM((R, C_KV_LORA), BF16), pltpu.VMEM((R, C_ROPE), BF16),
                            pltpu.VMEM((R, 1), F32), pltpu.VMEM((R, 1), F32),
                            pltpu.VMEM((R, C_KV_LORA), F32)]),
        out_shape=jax.ShapeDtypeStruct((C_HEADS, Ms, C_KV_LORA), F32),
        compiler_params=_params("parallel", "arbitrary"), name="mla_s_attend",
    )(page_table, q_lat, q, latn, krn, *([pool_lat] * n_pg), *([jnp.swapaxes(pool_kr, 2, 3)] * n_pg))

    return pl.pallas_call(
        _pair_matmul_kernel, grid=(C_HEADS // 2,),
        in_specs=[pl.BlockSpec((1, Ms, C_KV_LORA), lambda h: (2 * h, 0, 0)),
                  pl.BlockSpec((1, Ms, C_KV_LORA), lambda h: (2 * h + 1, 0, 0)),
                  pl.BlockSpec((1, C_KV_LORA, LANES), lambda h: (2 * h, 0, 0)),
                  pl.BlockSpec((1, C_KV_LORA, LANES), lambda h: (2 * h + 1, 0, 0))],
        out_specs=pl.BlockSpec((Ms, LANES), lambda h: (0, h)),
        out_shape=jax.ShapeDtypeStruct((Ms, C_HEADS * C_V), BF16),
        compiler_params=_params("parallel"), name="mla_s_out",
    )(o_lat, o_lat, w_uv, w_uv)


def _mla_weights(w_down, w_uq, w_ukv):
    D = w_down.shape[0]
    kr_cols = w_down[:, C_Q_LORA + C_KV_LORA:]
    wd = jnp.concatenate([w_down[:, :C_Q_LORA + C_KV_LORA], jnp.zeros((D, C_NOPE), F32), kr_cols,
                          jnp.zeros((D, LANES - C_NOPE - C_ROPE), F32)], axis=1)
    wq = jnp.pad(w_uq, ((0, 0), (0, 0), (0, LANES - C_NOPE - C_ROPE))).reshape(C_Q_LORA, C_HEADS * LANES)
    wkv = w_ukv.reshape(C_KV_LORA, C_HEADS * LANES)
    return wd.astype(BF16), wq.astype(BF16), wkv.astype(BF16)


def _mla_project(x, g_mix, wd, wq, wkv, g_q, g_kv, rope, tm, qdtype):
    cq, ckv, kr = _proj(x, wd, ((C_Q_LORA, False, F32), (C_KV_LORA, False, F32), (LANES, True, F32)),
                        gain=g_mix, rope=rope, tm=tm, name="mla_down")
    (q,) = _proj(cq, wq, ((C_HEADS * LANES, True, qdtype),), gain=g_q, rope=rope, tm=tm, name="mla_uq")
    kv, lat = _proj(ckv, wkv, ((C_HEADS * LANES, False, BF16),), gain=g_kv, emit_norm=True, tm=tm,
                    name="mla_ukv")
    return q, kv, lat, kr


def _even_weights(w_in):
    D = w_in.shape[0]
    sizes = (A_HEADS * A_HEAD_DIM, A_KV_HEADS * A_HEAD_DIM, A_KV_HEADS * A_HEAD_DIM,
             IDX_HEADS * IDX_DIM, IDX_DIM, IDX_HEADS, GDN_QK, GDN_QK, GDN_QK, GDN_QK, B_HEADS, B_HEADS)
    parts, c = [], 0
    for s in sizes:
        parts.append(w_in[:, c:c + s])
        c += s
    aq, ak, av, iq, ik, iw, bq, bk, bv, bz, bb, ba = parts
    zero = jnp.zeros((D, A_HEADS, A_HEAD_DIM), F32)
    aq = aq.reshape(D, A_HEADS, A_HEAD_DIM)
    in_g0 = (jnp.arange(A_HEADS) < A_HEADS // A_KV_HEADS)[None, :, None]
    aq128 = jnp.concatenate([jnp.where(in_g0, aq, zero), jnp.where(in_g0, zero, aq)], axis=-1)
    iq128 = jnp.concatenate([iq.reshape(D, IDX_HEADS, IDX_DIM), zero], axis=-1)
    ik128 = jnp.pad(ik, ((0, 0), (0, LANES - IDX_DIM)))
    misc = jnp.pad(jnp.concatenate([iw, bb, ba], axis=1), ((0, 0), (0, LANES - IDX_HEADS - 2 * B_HEADS)))
    w = jnp.concatenate([aq128.reshape(D, -1), ak, iq128.reshape(D, -1), ik128, av, misc, bq, bk, bv, bz],
                        axis=1)
    misc_col = A_HEADS * LANES + LANES + IDX_HEADS * LANES + LANES + LANES
    scale = jnp.ones((w.shape[1],), F32).at[misc_col:misc_col + IDX_HEADS].set(IDX_W_SCALE)
    assert math.frexp(A_SCALE)[0] == 0.5
    scale = scale.at[:A_HEADS * LANES].set(A_SCALE)
    return w.astype(BF16), scale


def _even_groups(qdtype, kdtype, t_rows):
    return ((A_HEADS * LANES, True, qdtype), (LANES, True, kdtype, t_rows),
            (IDX_HEADS * LANES, True, qdtype), (LANES, True, kdtype, t_rows),
            (LANES, False, None if t_rows else F32, t_rows), (LANES, False, F32),
            (GDN_CONV_DIM, False, F32), (GDN_QK, False, F32))


PROMPT_ROW_TILE = 512


def _tile_rows(t, rows):
    return jnp.tile(t, (rows // t.shape[0], 1))


def kernel(x_prompt, x_sample, cache_a_k, cache_a_v, cache_a_idx, state_b_ssm, state_b_conv,
           cache_c_latent, cache_c_krope, cache_mem_k, cache_mem_v, page_table, mem_prompt,
           g_mix, g_cross, g_mem, g_mlp, g_final,
           w_in_even, gdn_conv_w, gdn_a_log, gdn_dt_bias, gdn_norm_g, w_out_even,
           w_down_odd, g_q_lora, g_kv_lora, w_uq, w_ukv, w_out_odd,
           w_xq, w_xk, w_xv, w_xo, w_ff1, w_ff2):
    Bp, S, D = x_prompt.shape
    Bd, T, _ = x_sample.shape
    depth = g_mix.shape[0]
    Mp, Ms = Bp * S, Bd * T
    past_len = page_table.shape[1] * cache_a_k.shape[2]
    pos_p = jnp.arange(S, dtype=jnp.int32)
    pos_s = past_len + jnp.arange(T, dtype=jnp.int32)
    tm_p = min(PROMPT_ROW_TILE, S)
    tm_s = min(256, Ms)

    def rope_pair(dim, lo, hi):
        cp, sap, sbp, half = _rope_tables(pos_p, dim, lo, hi)
        cs, sas, sbs, _ = _rope_tables(pos_s, dim, lo, hi)
        return ((cp, sap, sbp, half, S // tm_p),
                (_tile_rows(cs, tm_s), _tile_rows(sas, tm_s), _tile_rows(sbs, tm_s), half, 1))

    rope_even_p, rope_even_s = rope_pair(A_HEAD_DIM, 0, LANES)
    rope_odd_p, rope_odd_s = rope_pair(C_ROPE, C_NOPE, C_NOPE + C_ROPE)

    xp = x_prompt.reshape(Mp, D)
    xs = x_sample.reshape(Ms, D)
    mem = mem_prompt.reshape(Bp * N_MEM, D)
    res = ((D, False, F32),)
    pa_k, pa_v, pa_i, pb_s, pb_c, pc_l, pc_r, pm_k, pm_v = [], [], [], [], [], [], [], [], []
    sa_k, sa_v, sa_i, sb_s, sb_c, sc_l, sc_r = [], [], [], [], [], [], []
    for li in range(depth):
        if li % 2 == 0:
            e = li // 2
            w_even, cscale = _even_weights(w_in_even[e])
            w_out = w_out_even[e].astype(BF16)
            gdn_w = (gdn_conv_w[e], gdn_a_log[e], gdn_dt_bias[e], gdn_norm_g[e])
            aq, ak, iq, ik, misc, conv_in, bz, ak_t, ik_t, av_t = _proj(
                xp, w_even, _even_groups(BF16, BF16, S), gain=g_mix[li], rope=rope_even_p,
                colscale=cscale, tm=tm_p, name="even_in")
            a_out = _dsa_prompt(iq, ik, misc[:, :IDX_HEADS].T, aq, ak, av_t, B=Bp, S=S)
            b_out, tail, sbd = _gdn(
                conv_in, bz, misc, jnp.zeros((Bp, GDN_TAIL, GDN_CONV_DIM), F32),
                jnp.zeros((Bp, B_HEADS // 2, LANES, LANES), F32), *gdn_w, B=Bp, L=S)
            (xp,) = _proj(jnp.concatenate([a_out, b_out], axis=1), w_out, res, residual=xp, tm=tm_p,
                          name="even_out")
            from_t = lambda a: jnp.transpose(a.reshape(Bp, A_KV_HEADS, A_HEAD_DIM, S), (0, 3, 1, 2))
            pa_k.append(from_t(ak_t))
            pa_v.append(from_t(av_t))
            pa_i.append(jnp.swapaxes(ik_t[:, :IDX_DIM, :], 1, 2))
            pb_s.append(_state_from_blockdiag(sbd))
            pb_c.append(tail[:, GDN_TAIL - (CONV_WIDTH - 1):])
            aq, ak, iq, ik, av, misc, conv_in, bz = _proj(
                xs, w_even, _even_groups(F32, F32, 0), gain=g_mix[li], rope=rope_even_s, colscale=cscale,
                tm=tm_s, name="even_in_s")
            a_out = _dsa_sample(iq, ik, misc, aq, ak, av, cache_a_k, cache_a_v, cache_a_idx,
                                page_table, e, Bd=Bd, T=T)
            tail0 = jnp.pad(state_b_conv[e], ((0, 0), (GDN_TAIL - (CONV_WIDTH - 1), 0), (0, 0)))
            b_out, tail, sbd = _gdn(conv_in, bz, misc, tail0, _state_to_blockdiag(state_b_ssm[e]),
                                    *gdn_w, B=Bd, L=T)
            (xs,) = _proj(jnp.concatenate([a_out, b_out], axis=1), w_out, res, residual=xs, tm=tm_s,
                          name="even_out_s")
            sa_k.append(ak.reshape(Bd, T, A_KV_HEADS, A_HEAD_DIM))
            sa_v.append(av.reshape(Bd, T, A_KV_HEADS, A_HEAD_DIM))
            sa_i.append(ik[:, :IDX_DIM].reshape(Bd, T, IDX_DIM))
            sb_s.append(_state_from_blockdiag(sbd).astype(state_b_ssm.dtype))
            sb_c.append(tail[:, GDN_TAIL - (CONV_WIDTH - 1):])
        else:
            o = li // 2
            wd, wq, wkv = _mla_weights(w_down_odd[o], w_uq[o], w_ukv[o])
            w_out = w_out_odd[o].astype(BF16)
            q, kv, lat, kr = _mla_project(xp, g_mix[li], wd, wq, wkv, g_q_lora[o], g_kv_lora[o],
                                          rope_odd_p, tm_p, BF16)
            (xp,) = _proj(_mla_prompt(q, kv, kr, B=Bp, S=S), w_out, res, residual=xp, tm=tm_p,
                          name="odd_out")
            pc_l.append(lat.reshape(Bp, S, C_KV_LORA))
            pc_r.append(kr[:, C_NOPE:C_NOPE + C_ROPE].reshape(Bp, S, C_ROPE))
            q, kv, lat, kr = _mla_project(xs, g_mix[li], wd, wq, wkv, g_q_lora[o], g_kv_lora[o],
                                          rope_odd_s, tm_s, F32)
            a_out = _mla_sample(q, lat, kr, w_ukv[o], cache_c_latent, cache_c_krope, page_table, o,
                                Bd=Bd, T=T)
            (xs,) = _proj(a_out, w_out, res, residual=xs, tm=tm_s, name="odd_out_s")
            sc_l.append(lat.reshape(Bd, T, C_KV_LORA))
            sc_r.append(kr[:, C_NOPE:C_NOPE + C_ROPE].reshape(Bd, T, C_ROPE))
        w_kv = jnp.concatenate([w_xk[li], w_xv[li]], axis=1).astype(BF16)
        mk, mv = _proj(mem, w_kv, ((X_WIDTH, False, F32), (X_WIDTH, False, F32)), gain=g_mem[li],
                       tm=min(256, mem.shape[0]), name="mem_kv")
        pm_k.append(mk.reshape(Bp, N_MEM, X_HEADS, X_HEAD_DIM))
        pm_v.append(mv.reshape(Bp, N_MEM, X_HEADS, X_HEAD_DIM))
        wxq, wxo = w_xq[li].astype(BF16), w_xo[li].astype(BF16)
        xp = _xattn(xp, g_cross[li], wxq, mk.reshape(Bp, N_MEM, X_WIDTH), mv.reshape(Bp, N_MEM, X_WIDTH),
                    wxo, rows_per_batch=S, tm=tm_p)
        xs = _xattn_rows(xs, g_cross[li], wxq, cache_mem_k, cache_mem_v, wxo, li, T=T,
                         G=math.gcd(Bd, XATTN_ROWS_PER_STEP))
        w1, w2 = w_ff1[li].astype(BF16), w_ff2[li].astype(BF16)
        last = li == depth - 1
        xp = _mlp(xp, g_mlp[li], w1, w2, g_final, final_norm=last)
        xs = _mlp(xs, g_mlp[li], w1, w2, g_final, final_norm=last)
    return (xp.reshape(Bp, S, D), xs.reshape(Bd, T, D),
            jnp.stack(pa_k, axis=1), jnp.stack(pa_v, axis=1), jnp.stack(pa_i, axis=1),
            jnp.stack(pb_s, axis=0), jnp.stack(pb_c, axis=0),
            jnp.stack(pc_l, axis=1), jnp.stack(pc_r, axis=1),
            jnp.stack(pm_k, axis=0), jnp.stack(pm_v, axis=0),
            jnp.stack(sa_k, axis=1), jnp.stack(sa_v, axis=1), jnp.stack(sa_i, axis=1),
            jnp.stack(sb_s, axis=0), jnp.stack(sb_c, axis=0),
            jnp.stack(sc_l, axis=1), jnp.stack(sc_r, axis=1))
```

```python
import functools
import math

import jax
import jax.numpy as jnp
from jax import lax
from jax.experimental import pallas as pl
from jax.experimental.pallas import tpu as pltpu

F32 = jnp.float32
BF16 = jnp.bfloat16
I32 = jnp.int32

LANES = 128
SUBLANES = 8
VMEM_LIMIT = 56 * 1024 * 1024

EPS = 1e-6
ROPE_THETA = 10000.0
N_MEM = 256
TOPK_MAX = 256

A_HEADS = 8
A_KV_HEADS = 2
A_HEAD_DIM = 64
IDX_HEADS = 8
IDX_DIM = 64
IDX_W_SCALE = (IDX_HEADS * IDX_DIM) ** -0.5
A_SCALE = A_HEAD_DIM ** -0.5

B_HEADS = 8
B_HEAD_DIM = 64
CONV_WIDTH = 4
GDN_CHUNK = 64
GDN_QK = B_HEADS * B_HEAD_DIM
GDN_CONV_DIM = 3 * GDN_QK

C_HEADS = 16
C_NOPE = 64
C_ROPE = 32
C_V = 64
C_Q_LORA = 384
C_KV_LORA = 256
MLA_SCALE = (C_NOPE + C_ROPE) ** -0.5
LOG2_E = math.log2(math.e)

X_HEADS = 4
X_HEAD_DIM = 128
X_WIDTH = X_HEADS * X_HEAD_DIM
X_SCALE = X_HEAD_DIM ** -0.5

INT_MIN = -2 ** 31
NEG_BIG = -1e30


def _params(*sem):
    return pltpu.CompilerParams(dimension_semantics=sem, vmem_limit_bytes=VMEM_LIMIT)


def _rms(x, g):
    return x * lax.rsqrt(jnp.mean(x * x, axis=-1, keepdims=True) + EPS) * g


def _dot(a, b):
    return jnp.dot(a, b, preferred_element_type=F32)


def _dot_nt(a, b):
    return lax.dot_general(a, b, (((1,), (1,)), ((), ())), preferred_element_type=F32)


def _split3(a):
    a1 = a.astype(BF16)
    r = a - a1.astype(F32)
    a2 = r.astype(BF16)
    a3 = (r - a2.astype(F32)).astype(BF16)
    return a1, a2, a3


def _dot_hi(a, b, nt=False):
    f = _dot_nt if nt else _dot
    a1, a2, a3 = _split3(a)
    b1, b2, b3 = _split3(b)
    small = f(a1, b3) + f(a3, b1) + f(a2, b2)
    mid = f(a1, b2) + f(a2, b1)
    return f(a1, b1) + (mid + small)


def _dot_exact_rhs(a, b_bf16):
    a1, a2, a3 = _split3(a)
    return _dot(a1, b_bf16) + (_dot(a2, b_bf16) + _dot(a3, b_bf16))


def _proj_kernel(*refs, groups, has_gain, has_rope, has_scale, has_res, emit_norm, shift):
    it = iter(refs)
    x_ref = next(it)
    g_ref = next(it) if has_gain else None
    w_ref = next(it)
    if has_rope:
        c_ref, sa_ref, sb_ref = next(it), next(it), next(it)
    s_ref = next(it) if has_scale else None
    r_ref = next(it) if has_res else None
    outs = list(it)
    x = x_ref[...].astype(F32)
    if has_gain:
        x = _rms(x, g_ref[...])
    if emit_norm:
        outs[-1][...] = x
    xb = x.astype(BF16)
    col = 0
    plain = iter(outs[:sum(1 for g in groups if g[2] is not None)])
    flipped = iter(outs[sum(1 for g in groups if g[2] is not None):])
    for width, rope, dtype, t_rows in groups:
        y = _dot(xb, w_ref[:, col:col + width])
        if has_scale:
            y = y * s_ref[:, col:col + width]
        if rope:
            n = width // LANES
            c = jnp.concatenate([c_ref[...]] * n, axis=1)
            sa = jnp.concatenate([sa_ref[...]] * n, axis=1)
            sb = jnp.concatenate([sb_ref[...]] * n, axis=1)
            y = y * c + pltpu.roll(y, shift, 1) * sa + pltpu.roll(y, width - shift, 1) * sb
        if has_res:
            y = y + r_ref[...]
        if dtype is not None:
            next(plain)[...] = y.astype(dtype)
        if t_rows:
            next(flipped)[0] = y.T
        col += width


def _proj(x, w, groups, *, gain=None, rope=None, colscale=None, residual=None,
          emit_norm=False, tm=256, name="proj"):
    M, K = x.shape
    N = w.shape[1]
    groups = tuple(tuple(g) + (0,) * (4 - len(g)) for g in groups)
    assert sum(g[0] for g in groups) == N and M % tm == 0
    args, specs = [x], [pl.BlockSpec((tm, K), lambda i: (i, 0))]
    if gain is not None:
        args.append(gain.reshape(1, K).astype(F32))
        specs.append(pl.BlockSpec((1, K), lambda i: (0, 0)))
    args.append(w)
    specs.append(pl.BlockSpec((K, N), lambda i: (0, 0)))
    shift = 0
    if rope is not None:
        c, sa, sb, shift, nblk = rope
        for t in (c, sa, sb):
            args.append(t)
            specs.append(pl.BlockSpec((tm, LANES), lambda i, nblk=nblk: (i % nblk, 0)))
    if colscale is not None:
        args.append(colscale.reshape(1, N).astype(F32))
        specs.append(pl.BlockSpec((1, N), lambda i: (0, 0)))
    if residual is not None:
        assert len(groups) == 1
        args.append(residual)
        specs.append(pl.BlockSpec((tm, N), lambda i: (i, 0)))
    out_shape = [jax.ShapeDtypeStruct((M, g[0]), g[2]) for g in groups if g[2] is not None]
    out_specs = [pl.BlockSpec((tm, g[0]), lambda i: (i, 0)) for g in groups if g[2] is not None]
    for width, _, _, t_rows in groups:
        if t_rows:
            per = t_rows // tm
            assert t_rows % tm == 0
            out_shape.append(jax.ShapeDtypeStruct((M // t_rows, width, t_rows), F32))
            out_specs.append(pl.BlockSpec((1, width, tm), lambda i, per=per: (i // per, 0, i % per)))
    if emit_norm:
        out_shape.append(jax.ShapeDtypeStruct((M, K), F32))
        out_specs.append(pl.BlockSpec((tm, K), lambda i: (i, 0)))
    kern = functools.partial(
        _proj_kernel, groups=groups, has_gain=gain is not None,
        has_rope=rope is not None, has_scale=colscale is not None,
        has_res=residual is not None, emit_norm=emit_norm, shift=shift)
    return pl.pallas_call(
        kern, grid=(M // tm,), in_specs=specs, out_specs=out_specs,
        out_shape=out_shape, compiler_params=_params("parallel"), name=name)(*args)


def _mlp_kernel(x_ref, g_ref, w1_ref, w2_ref, gf_ref, o_ref, hn_ref, acc_ref, *, final_norm):
    j = pl.program_id(1)

    @pl.when(j == 0)
    def _():
        hn_ref[...] = _rms(x_ref[...], g_ref[...]).astype(BF16)
        acc_ref[...] = jnp.zeros_like(acc_ref)

    a = _dot(hn_ref[...], w1_ref[...])
    a = jnp.square(jnp.maximum(a, 0.0)).astype(BF16)
    acc_ref[...] += _dot(a, w2_ref[...])

    @pl.when(j == pl.num_programs(1) - 1)
    def _():
        y = x_ref[...] + acc_ref[...]
        if final_norm:
            y = _rms(y, gf_ref[...])
        o_ref[...] = y


def _mlp(x, g, w1, w2, g_final, *, final_norm, tm=512, tf=1024):
    M, D = x.shape
    F = w1.shape[1]
    tm = min(tm, M)
    assert M % tm == 0 and F % tf == 0
    return pl.pallas_call(
        functools.partial(_mlp_kernel, final_norm=final_norm),
        grid=(M // tm, F // tf),
        in_specs=[pl.BlockSpec((tm, D), lambda i, j: (i, 0)),
                  pl.BlockSpec((1, D), lambda i, j: (0, 0)),
                  pl.BlockSpec((D, tf), lambda i, j: (0, j)),
                  pl.BlockSpec((tf, D), lambda i, j: (j, 0)),
                  pl.BlockSpec((1, D), lambda i, j: (0, 0))],
        out_specs=pl.BlockSpec((tm, D), lambda i, j: (i, 0)),
        out_shape=jax.ShapeDtypeStruct((M, D), F32),
        scratch_shapes=[pltpu.VMEM((tm, D), BF16), pltpu.VMEM((tm, D), F32)],
        compiler_params=_params("parallel", "arbitrary"), name="mlp",
    )(x, g.reshape(1, D), w1, w2, g_final.reshape(1, D))


def _xattn_kernel(x_ref, g_ref, wq_ref, mk_ref, mv_ref, wo_ref, o_ref):
    x = x_ref[...]
    q = _dot(_rms(x, g_ref[...]).astype(BF16), wq_ref[...])
    mk = mk_ref[0].astype(BF16)
    mv = mv_ref[0].astype(BF16)
    heads = []
    for h in range(X_HEADS):
        sl = slice(h * X_HEAD_DIM, (h + 1) * X_HEAD_DIM)
        s = _dot_nt(q[:, sl].astype(BF16), mk[:, sl]) * X_SCALE
        s = s - jnp.max(s, axis=-1, keepdims=True)
        p = jnp.exp(s)
        p = p / jnp.sum(p, axis=-1, keepdims=True)
        heads.append(_dot(p.astype(BF16), mv[:, sl]))
    o = jnp.concatenate(heads, axis=1).astype(BF16)
    o_ref[...] = x + _dot(o, wo_ref[...])


def _xattn(x, g, wq, mk, mv, wo, *, rows_per_batch, tm):
    M, D = x.shape
    assert rows_per_batch % tm == 0
    per = rows_per_batch // tm
    nm = mk.shape[1]
    return pl.pallas_call(
        _xattn_kernel, grid=(M // tm,),
        in_specs=[pl.BlockSpec((tm, D), lambda i: (i, 0)),
                  pl.BlockSpec((1, D), lambda i: (0, 0)),
                  pl.BlockSpec((D, X_WIDTH), lambda i: (0, 0)),
                  pl.BlockSpec((1, nm, X_WIDTH), lambda i: (i // per, 0, 0)),
                  pl.BlockSpec((1, nm, X_WIDTH), lambda i: (i // per, 0, 0)),
                  pl.BlockSpec((X_WIDTH, D), lambda i: (0, 0))],
        out_specs=pl.BlockSpec((tm, D), lambda i: (i, 0)),
        out_shape=jax.ShapeDtypeStruct((M, D), F32),
        compiler_params=_params("parallel"), name="xattn",
    )(x, g.reshape(1, D), wq, mk, mv, wo)


def _xattn_rows_kernel(x_ref, g_ref, wq_ref, mk_ref, mv_ref, wo_ref, o_ref, *, G, T):
    x = x_ref[...]
    q = _dot(_rms(x, g_ref[...]).astype(BF16), wq_ref[...])
    rows = mk_ref.shape[2]
    own = (lax.broadcasted_iota(I32, (X_HEADS * T, rows), 1) % X_HEADS
           == lax.broadcasted_iota(I32, (X_HEADS * T, rows), 0) // T)
    outs = []
    for g in range(G):
        qs = _stack_heads(q[g * T:(g + 1) * T], X_HEADS, X_HEAD_DIM).astype(BF16)
        s = jnp.where(own, _dot_nt(qs, mk_ref[0, g].astype(BF16)) * X_SCALE, -jnp.inf)
        p = jnp.exp(s - jnp.max(s, axis=-1, keepdims=True))
        p = p / jnp.sum(p, axis=-1, keepdims=True)
        o = _dot(p.astype(BF16), mv_ref[0, g].astype(BF16))
        outs.append(jnp.concatenate([o[h * T:(h + 1) * T] for h in range(X_HEADS)], axis=1))
    o_all = jnp.concatenate(outs, axis=0).astype(BF16)
    o_ref[...] = x + _dot(o_all, wo_ref[...])


def _xattn_rows(x, g, wq, mk, mv, wo, layer, *, T, G):
    M, D = x.shape
    B = M // T
    assert B % G == 0
    rows = mk.shape[2] * X_HEADS
    mk = mk.reshape(mk.shape[0], B, rows, X_HEAD_DIM)
    mv = mv.reshape(mv.shape[0], B, rows, X_HEAD_DIM)
    return pl.pallas_call(
        functools.partial(_xattn_rows_kernel, G=G, T=T), grid=(B // G,),
        in_specs=[pl.BlockSpec((G * T, D), lambda i: (i, 0)),
                  pl.BlockSpec((1, D), lambda i: (0, 0)),
                  pl.BlockSpec((D, X_WIDTH), lambda i: (0, 0)),
                  pl.BlockSpec((1, G, rows, X_HEAD_DIM), lambda i: (layer, i, 0, 0)),
                  pl.BlockSpec((1, G, rows, X_HEAD_DIM), lambda i: (layer, i, 0, 0)),
                  pl.BlockSpec((X_WIDTH, D), lambda i: (0, 0))],
        out_specs=pl.BlockSpec((G * T, D), lambda i: (i, 0)),
        out_shape=jax.ShapeDtypeStruct((M, D), F32),
        compiler_params=_params("parallel"), name="xattn_rows",
    )(x, g.reshape(1, D), wq, mk, mv, wo)


def _rope_tables(pos, dim, lane_lo, lane_hi):
    half = dim // 2
    lane = jnp.arange(LANES)
    j = (lane - lane_lo) % dim
    inside = (lane >= lane_lo) & (lane < lane_hi)
    inv = jnp.exp(-math.log(ROPE_THETA) * (j % half).astype(F32) / half)
    ang = pos.astype(F32)[:, None] * inv[None, :]
    cos, sin = jnp.cos(ang), jnp.sin(ang)
    c = jnp.where(inside[None, :], cos, 1.0)
    sa = jnp.where((inside & (j >= half))[None, :], sin, 0.0)
    sb = jnp.where((inside & (j < half))[None, :], -sin, 0.0)
    return c, sa, sb, half


def _mla_prompt_block(q_ref, kv_ref, kr_ref, o_ref, n, tq):
    lo = n * tq
    row = lax.broadcasted_iota(I32, (tq, tq), 0)
    col = lax.broadcasted_iota(I32, (tq, tq), 1)

    def keys(a, b, h):
        kv = kv_ref[a:b, h * LANES:(h + 1) * LANES]
        lane = lax.broadcasted_iota(I32, (b - a, LANES), 1)
        return jnp.where(lane < C_NOPE, kv, kr_ref[a:b, :].astype(kv.dtype)), kv

    outs = []
    for h in range(2):
        q = q_ref[:, h * LANES:(h + 1) * LANES]
        kd, vd = keys(lo, lo + tq, h)
        sd = jnp.where(col <= row, _dot_nt(q, kd), -jnp.inf)
        m = jnp.max(sd, axis=-1, keepdims=True)
        if n > 0:
            ka, va = keys(0, lo, h)
            sa = _dot_nt(q, ka)
            m = jnp.maximum(m, jnp.max(sa, axis=-1, keepdims=True))
        pd = jnp.exp2((sd - m) * (MLA_SCALE * LOG2_E))
        l = jnp.sum(pd, axis=-1, keepdims=True)
        acc = _dot(pd.astype(BF16), vd)
        if n > 0:
            pa = jnp.exp2((sa - m) * (MLA_SCALE * LOG2_E))
            l = l + jnp.sum(pa, axis=-1, keepdims=True)
            acc = acc + _dot(pa.astype(BF16), va)
        outs.append(acc / l)
    lane_o = lax.broadcasted_iota(I32, (tq, LANES), 1)
    o_ref[...] = jnp.where(lane_o < C_V, pltpu.roll(outs[0], C_V, 1), outs[1]).astype(o_ref.dtype)


def _mla_prompt_kernel(q_ref, kv_ref, kr_ref, o_ref, *, tq, nq):
    i = pl.program_id(2)
    for n in range(nq):
        pl.when(i == n)(functools.partial(_mla_prompt_block, q_ref, kv_ref, kr_ref, o_ref, n, tq))


def _mla_prompt(q, kv, kr, *, B, S, tq=512):
    tq = min(tq, S)
    nq = S // tq
    return pl.pallas_call(
        functools.partial(_mla_prompt_kernel, tq=tq, nq=nq),
        grid=(B, C_HEADS // 2, nq),
        in_specs=[pl.BlockSpec((tq, 2 * LANES), lambda b, h, i: (b * nq + i, h)),
                  pl.BlockSpec((S, 2 * LANES), lambda b, h, i: (b, h)),
                  pl.BlockSpec((S, LANES), lambda b, h, i: (b, 0))],
        out_specs=pl.BlockSpec((tq, LANES), lambda b, h, i: (b * nq + i, h)),
        out_shape=jax.ShapeDtypeStruct((B * S, C_HEADS * C_V), BF16),
        compiler_params=_params("parallel", "parallel", "arbitrary"), name="mla_prompt",
    )(q, kv, kr)


def _order_key(sc):
    sc = jnp.where(sc == 0.0, 0.0, sc)
    bits = lax.bitcast_convert_type(sc, I32)
    return bits ^ ((bits >> 31) & 0x7FFFFFFF)


def _topk_search(count, k, n_idx_bits):
    kf = float(k)
    t0 = jnp.where(count(lambda key, idx: key >= 0) >= kf, 0, INT_MIN).astype(I32)

    def vbody(n, t):
        cand = t | jnp.left_shift(jnp.int32(1), 30 - n)
        return jnp.where(count(lambda key, idx: key >= cand) >= kf, cand, t)

    t = lax.fori_loop(0, 31, vbody, t0)
    need = kf - count(lambda key, idx: key > t)
    n_eq = count(lambda key, idx: key == t)

    def search():
        def ibody(n, x):
            cand = x | jnp.left_shift(jnp.int32(1), n_idx_bits - 1 - n)
            return jnp.where(count(lambda key, idx: (key == t) & (idx < cand)) < need, cand, x)

        return lax.fori_loop(0, n_idx_bits, ibody, jnp.zeros_like(t))

    tied = jnp.max(jnp.where((n_eq > need) & (t > INT_MIN), 1.0, 0.0)) > 0.5
    x = lax.cond(tied, search, lambda: jnp.full_like(t, (1 << n_idx_bits) - 1))
    return t, x


def _topk_mask(key, idx, k, axis, n_idx_bits):
    def count(pred):
        return jnp.sum(pred(key, idx).astype(F32), axis=axis, keepdims=True)

    t, x = _topk_search(count, k, n_idx_bits)
    return (key > t) | ((key == t) & (idx <= x))


def _dsa_prompt_kernel(iq_ref, ik_ref, iw_ref, aq_ref, ak_ref, avt_ref, o_ref,
                       key_ref, m_ref, l_ref, acc_ref, *, tq, topk):
    i = pl.program_id(1)
    S = ik_ref.shape[0]
    n_chunks = i + 1
    kofs = lax.broadcasted_iota(I32, (tq, tq), 0)
    qidx = i * tq + lax.broadcasted_iota(I32, (tq, tq), 1)

    def chunk_start(c):
        return pl.multiple_of(c * tq, tq)

    def score_chunk(c, carry):
        ik = ik_ref[pl.ds(chunk_start(c), tq), :].astype(BF16)
        sc = jnp.zeros((tq, tq), F32)
        for h in range(IDX_HEADS):
            s = _dot_nt(ik, iq_ref[:, h * LANES:(h + 1) * LANES])
            sc = sc + iw_ref[h:h + 1, :] * jnp.maximum(s, 0.0)
        key_ref[pl.ds(chunk_start(c), tq), :] = jnp.where(kofs + c * tq <= qidx, _order_key(sc), INT_MIN)
        return carry

    lax.fori_loop(0, n_chunks, score_chunk, 0)

    def count(pred):
        def body(c, acc):
            hit = pred(key_ref[pl.ds(chunk_start(c), tq), :], kofs + c * tq)
            return acc + jnp.sum(hit.astype(F32).reshape(tq // SUBLANES, SUBLANES, tq), axis=0)

        part = lax.fori_loop(0, n_chunks, body, jnp.zeros((SUBLANES, tq), F32))
        return jnp.sum(part, axis=0, keepdims=True)

    t, x = _topk_search(count, topk, max(1, (S - 1).bit_length()))

    m_ref[...] = jnp.full(m_ref.shape, NEG_BIG, F32)
    l_ref[...] = jnp.zeros(l_ref.shape, F32)
    acc_ref[...] = jnp.zeros(acc_ref.shape, F32)
    rep = A_HEADS // A_KV_HEADS

    def attend_chunk(c, carry):
        start = chunk_start(c)
        key = key_ref[pl.ds(start, tq), :]
        kidx = kofs + c * tq
        sel = ((key > t) | ((key == t) & (kidx <= x))) & (kidx <= qidx)
        ak = ak_ref[pl.ds(start, tq), :].astype(BF16)
        avt = avt_ref[0, :, pl.ds(start, tq)].astype(BF16)
        heads = range(A_HEADS)
        s = [jnp.where(sel, _dot_nt(ak, aq_ref[:, h * LANES:(h + 1) * LANES]), NEG_BIG) for h in heads]
        m_old = m_ref[...]
        m_new = jnp.maximum(m_old, jnp.concatenate([jnp.max(s[h], axis=0, keepdims=True) for h in heads],
                                                   axis=0))
        corr = jnp.exp(m_old - m_new)
        p = [jnp.exp(s[h] - m_new[h:h + 1, :]) for h in heads]
        l_ref[...] = l_ref[...] * corr + jnp.concatenate(
            [jnp.sum(p[h], axis=0, keepdims=True) for h in heads], axis=0)
        pv = [_dot(avt[(h // rep) * A_HEAD_DIM:(h // rep + 1) * A_HEAD_DIM], p[h].astype(BF16))
              for h in heads]
        for h in heads:
            acc_ref[h] = acc_ref[h] * corr[h:h + 1, :] + pv[h]
        m_ref[...] = m_new
        return carry

    lax.fori_loop(0, n_chunks, attend_chunk, 0)
    outs = [acc_ref[h] / l_ref[h:h + 1, :] for h in range(A_HEADS)]
    o_ref[...] = jnp.concatenate(outs, axis=0).T.astype(o_ref.dtype)


def _dsa_prompt(iq, ik, iw_t, aq, ak, av_t, *, B, S, tq=256):
    tq = min(tq, S)
    nq = S // tq
    topk = min(TOPK_MAX, S // 4)
    W = A_HEADS * LANES
    return pl.pallas_call(
        functools.partial(_dsa_prompt_kernel, tq=tq, topk=topk),
        grid=(B, nq),
        scratch_shapes=[pltpu.VMEM((S, tq), I32), pltpu.VMEM((A_HEADS, tq), F32),
                        pltpu.VMEM((A_HEADS, tq), F32), pltpu.VMEM((A_HEADS, A_HEAD_DIM, tq), F32)],
        in_specs=[pl.BlockSpec((tq, W), lambda b, i: (b * nq + i, 0)),
                  pl.BlockSpec((S, LANES), lambda b, i: (b, 0)),
                  pl.BlockSpec((IDX_HEADS, tq), lambda b, i: (0, b * nq + i)),
                  pl.BlockSpec((tq, W), lambda b, i: (b * nq + i, 0)),
                  pl.BlockSpec((S, LANES), lambda b, i: (b, 0)),
                  pl.BlockSpec((1, LANES, S), lambda b, i: (b, 0, 0))],
        out_specs=pl.BlockSpec((tq, A_HEADS * A_HEAD_DIM), lambda b, i: (b * nq + i, 0)),
        out_shape=jax.ShapeDtypeStruct((B * S, A_HEADS * A_HEAD_DIM), BF16),
        compiler_params=_params("parallel", "arbitrary"), name="dsa_prompt",
    )(iq, ik, iw_t, aq, ak, av_t)


GDN_BETA_LANE = 8
GDN_DECAY_LANE = 16
GDN_TAIL = SUBLANES


def _softplus(x):
    return jnp.maximum(x, 0.0) + jnp.log1p(jnp.exp(-jnp.abs(x)))


def _silu(x):
    return x * jax.nn.sigmoid(x)


def _dot3(a, b, nt=False):
    f = _dot_nt if nt else _dot
    a1 = a.astype(BF16)
    a2 = (a - a1.astype(F32)).astype(BF16)
    b1 = b.astype(BF16)
    b2 = (b - b1.astype(F32)).astype(BF16)
    return f(a1, b1) + (f(a1, b2) + f(a2, b1))


def _gdn_kernel(xin_ref, z_ref, misc_ref, misct_ref, tail0_ref, s0_ref, cw_ref, alog_ref, dtb_ref,
                alogt_ref, dtbt_ref, ng_ref, o_ref, tail_ref, sout_ref, xp_ref, s_ref, *, C, NB):
    c = pl.program_id(1)
    NP = B_HEADS // 2

    @pl.when(c == 0)
    def _():
        xp_ref[:, 0:GDN_TAIL, :] = tail0_ref[...]
        s_ref[...] = s0_ref[...]

    @pl.when(c > 0)
    def _():
        xp_ref[:, 0:GDN_TAIL, :] = xp_ref[:, C:C + GDN_TAIL, :]

    xp_ref[:, GDN_TAIL:GDN_TAIL + C, :] = xin_ref[...]
    tail_ref[...] = xp_ref[:, C:C + GDN_TAIL, :]
    base = GDN_TAIL - (CONV_WIDTH - 1)
    ti = lax.broadcasted_iota(I32, (C, C), 0)
    tj = lax.broadcasted_iota(I32, (C, C), 1)
    low = (tj <= ti).astype(BF16)
    upp = (ti <= tj).astype(BF16)

    first = lax.broadcasted_iota(I32, (C, LANES), 1) < B_HEAD_DIM
    li = lax.broadcasted_iota(I32, (LANES, LANES), 0)
    lj = lax.broadcasted_iota(I32, (LANES, LANES), 1)
    blockdiag = (li < B_HEAD_DIM) == (lj < B_HEAD_DIM)
    ones_bd = blockdiag.astype(BF16)
    own_lanes = ((lax.broadcasted_iota(I32, (2 * C, LANES), 0) < C)
                 == (lax.broadcasted_iota(I32, (2 * C, LANES), 1) < B_HEAD_DIM))
    ri = lax.broadcasted_iota(I32, (2 * C, 2 * C), 0)
    rj = lax.broadcasted_iota(I32, (2 * C, 2 * C), 1)
    same_head = (ri < C) == (rj < C)
    incl = same_head & (rj <= ri)
    strict = same_head & (rj < ri)
    eye = (ri == rj).astype(F32)

    def pair(col0, col1):
        return jnp.where(first, col0, col1)

    def halves(x):
        return pair(x[:C, :LANES], x[C:, :LANES])

    def twice(x):
        return jnp.concatenate([x, x], axis=0)

    st = []
    for bb, p in [(bb, p) for bb in range(NB) for p in range(NP)]:
        if p == 0:
            y = xp_ref[bb, base:base + C, :] * cw_ref[0:1, :]
            for i in range(1, CONV_WIDTH):
                y = y + xp_ref[bb, base + i:base + i + C, :] * cw_ref[i:i + 1, :]
            xc = _silu(y)
            misc = misc_ref[bb]
            beta = jax.nn.sigmoid(misc)
            g = -jnp.exp(alog_ref[...]) * _softplus(misc + dtb_ref[...])
            gt = -jnp.exp(alogt_ref[...]) * _softplus(misct_ref[bb, 0] + dtbt_ref[...])
            g1, g2, g3 = _split3(g)
            gc = _dot(low, g1) + (_dot(low, g2) + _dot(low, g3))
            t1, t2, t3 = _split3(gt)
            gct = _dot(t1, upp) + (_dot(t2, upp) + _dot(t3, upp))
        sl = slice(p * LANES, (p + 1) * LANES)
        q2 = xc[:, sl]
        k2 = xc[:, GDN_QK + p * LANES:GDN_QK + (p + 1) * LANES]
        v2 = xc[:, 2 * GDN_QK + p * LANES:2 * GDN_QK + (p + 1) * LANES]
        q2 = q2 * lax.rsqrt(_dot_exact_rhs(q2 * q2, ones_bd) + EPS) * (B_HEAD_DIM ** -0.5)
        k2 = k2 * lax.rsqrt(_dot_exact_rhs(k2 * k2, ones_bd) + EPS)
        bcol = [beta[:, GDN_BETA_LANE + 2 * p + e:GDN_BETA_LANE + 2 * p + e + 1] for e in range(2)]
        gcol = [gc[:, GDN_DECAY_LANE + 2 * p + e:GDN_DECAY_LANE + 2 * p + e + 1] for e in range(2)]
        glast = [gcol[e][C - 1:C, :] for e in range(2)]
        beta2 = pair(bcol[0], bcol[1])
        gam2 = jnp.exp(pair(gcol[0], gcol[1]))
        kd2 = k2 * jnp.exp(pair(glast[0] - gcol[0], glast[1] - gcol[1]))
        gl2 = jnp.exp(jnp.where(first[0:1, :], glast[0], glast[1]))
        rhs = twice(jnp.concatenate([v2 * beta2, k2 * (beta2 * gam2)], axis=1))
        grow = jnp.concatenate([gct[GDN_DECAY_LANE + 2 * p + e:GDN_DECAY_LANE + 2 * p + e + 1, :]
                                for e in range(2)], axis=1)
        decay = jnp.exp(jnp.where(incl, jnp.concatenate(gcol, axis=0) - grow, -jnp.inf))
        kx = jnp.where(own_lanes, twice(k2), 0.0)
        qx = jnp.where(own_lanes, twice(q2), 0.0)
        a = jnp.where(strict, jnp.concatenate(bcol, axis=0) * _dot3(kx, kx, nt=True) * decay, 0.0)
        aqk = _dot_nt(qx.astype(BF16), kx.astype(BF16)) * decay
        st.append(dict(bb=bb, p=p, sl=sl, q2=q2, gam2=gam2, kd2=kd2, gl2=gl2, rhs=rhs, aqk=aqk, a=a))

    tinv = [eye - d["a"] for d in st]
    xk = [d["a"] for d in st]
    span = 2
    while span < C:
        xk = [_dot3(x, x) for x in xk]
        tinv = [t + _dot3(t, x) for t, x in zip(tinv, xk)]
        span *= 2
    sols = [_dot3(t, d["rhs"]) for t, d in zip(tinv, st)]

    for d, sol in zip(st, sols):
        bb, p = d["bb"], d["p"]
        u2 = halves(sol[:, :LANES])
        w2 = halves(sol[:, LANES:])
        s_old = s_ref[bb, p]
        sb = s_old.astype(BF16)
        delta = u2 - _dot(w2.astype(BF16), sb)
        db = delta.astype(BF16)
        o2 = _dot((d["q2"] * d["gam2"]).astype(BF16), sb) + halves(_dot(d["aqk"].astype(BF16), twice(db)))
        upd = lax.dot_general(d["kd2"].astype(BF16), db, (((0,), (0,)), ((), ())),
                              preferred_element_type=F32)
        s_new = s_old * d["gl2"] + jnp.where(blockdiag, upd, 0.0)
        s_ref[bb, p] = s_new
        sout_ref[bb, p] = s_new
        ms = _dot_exact_rhs(o2 * o2, ones_bd) * (1.0 / B_HEAD_DIM)
        o_ref[bb, :, d["sl"]] = (o2 * lax.rsqrt(ms + EPS) * ng_ref[...]
                                 * _silu(z_ref[bb, :, d["sl"]])).astype(o_ref.dtype)


GDN_BATCH_PER_STEP = 4


def _gdn(xin, z, misc, tail0, s0, conv_w, a_log, dt_bias, norm_g, *, B, L):
    C = math.gcd(L, GDN_CHUNK)
    n = L // C
    M = B * L
    NP = B_HEADS // 2
    NB = math.gcd(B, GDN_BATCH_PER_STEP)
    nrow = GDN_DECAY_LANE + B_HEADS
    misc_t = jnp.swapaxes(misc[:, :nrow].reshape(B, n, C, nrow), 2, 3)
    xin, z, misc = (a.reshape(B, L, a.shape[1]) for a in (xin, z, misc))
    lane_vec = lambda v: jnp.zeros((1, LANES), F32).at[0, GDN_DECAY_LANE:nrow].set(v)
    col_vec = lambda v: jnp.zeros((nrow, 1), F32).at[GDN_DECAY_LANE:, 0].set(v)
    cw = jnp.pad(conv_w, ((0, SUBLANES - CONV_WIDTH), (0, 0)))
    ng = jnp.tile(norm_g.reshape(1, B_HEAD_DIM), (1, 2))
    full = lambda shape: pl.BlockSpec(shape, lambda b, c: (0,) * len(shape))
    o, tail, state = pl.pallas_call(
        functools.partial(_gdn_kernel, C=C, NB=NB),
        grid=(B // NB, n),
        in_specs=[pl.BlockSpec((NB, C, GDN_CONV_DIM), lambda b, c: (b, c, 0)),
                  pl.BlockSpec((NB, C, GDN_QK), lambda b, c: (b, c, 0)),
                  pl.BlockSpec((NB, C, LANES), lambda b, c: (b, c, 0)),
                  pl.BlockSpec((NB, 1, nrow, C), lambda b, c: (b, c, 0, 0)),
                  pl.BlockSpec((NB, GDN_TAIL, GDN_CONV_DIM), lambda b, c: (b, 0, 0)),
                  pl.BlockSpec((NB, NP, LANES, LANES), lambda b, c: (b, 0, 0, 0)),
                  full((SUBLANES, GDN_CONV_DIM)), full((1, LANES)), full((1, LANES)),
                  full((nrow, 1)), full((nrow, 1)), full((1, LANES))],
        out_specs=[pl.BlockSpec((NB, C, GDN_QK), lambda b, c: (b, c, 0)),
                   pl.BlockSpec((NB, GDN_TAIL, GDN_CONV_DIM), lambda b, c: (b, 0, 0)),
                   pl.BlockSpec((NB, NP, LANES, LANES), lambda b, c: (b, 0, 0, 0))],
        out_shape=[jax.ShapeDtypeStruct((B, L, GDN_QK), BF16),
                   jax.ShapeDtypeStruct((B, GDN_TAIL, GDN_CONV_DIM), F32),
                   jax.ShapeDtypeStruct((B, NP, LANES, LANES), F32)],
        scratch_shapes=[pltpu.VMEM((NB, C + GDN_TAIL, GDN_CONV_DIM), F32),
                        pltpu.VMEM((NB, NP, LANES, LANES), F32)],
        compiler_params=_params("arbitrary", "arbitrary"), name="gdn",
    )(xin, z, misc, misc_t, tail0, s0, cw, lane_vec(a_log), lane_vec(dt_bias),
      col_vec(a_log), col_vec(dt_bias), ng)
    return o.reshape(M, GDN_QK), tail, state


def _state_to_blockdiag(s):
    B = s.shape[0]
    d = B_HEAD_DIM
    s = s.reshape(B, B_HEADS // 2, 2, d, d)
    z = jnp.zeros_like(s[:, :, 0])
    top = jnp.concatenate([s[:, :, 0], z], axis=-1)
    bot = jnp.concatenate([z, s[:, :, 1]], axis=-1)
    return jnp.concatenate([top, bot], axis=-2)


def _state_from_blockdiag(s):
    d = B_HEAD_DIM
    return jnp.stack([s[:, :, :d, :d], s[:, :, d:, d:]], axis=2).reshape(s.shape[0], B_HEADS, d, d)


PAGES_PER_STEP = 32
SCORE_PAGES_PER_STEP = 64
XATTN_ROWS_PER_STEP = 8
PAGES_PER_BLOCK = 8
SELECT_ROWS = 64


def _page_specs(block, layer, n):
    nd = len(block)

    def spec(p):
        return pl.BlockSpec(block, lambda b, j, pt, p=p: (pt[b, j * n + p], layer) + (0,) * (nd - 2))

    return [spec(p) for p in range(n)]


def _stack_heads(x, n_heads, width=LANES):
    return jnp.concatenate([x[:, h * width:(h + 1) * width] for h in range(n_heads)], axis=0)


def _pad_rows(x, rows):
    return jnp.concatenate([x, jnp.zeros((rows - x.shape[0], x.shape[1]), x.dtype)], axis=0)


def _dsa_s_score_kernel(pt_ref, iq_ref, misc_ref, *refs, n_pg, T):
    pages, o_ref = refs[:n_pg], refs[n_pg]
    q = _stack_heads(iq_ref[...], IDX_HEADS)[:, :IDX_DIM].astype(BF16)
    misc = misc_ref[...]
    w = jnp.concatenate([misc[:, h:h + 1] for h in range(IDX_HEADS)], axis=0)
    keys_t = jnp.concatenate([pages[p][0, 0].astype(BF16) for p in range(n_pg)], axis=1)
    r = jnp.maximum(_dot(q, keys_t), 0.0) * w
    sc = r[0:T]
    for h in range(1, IDX_HEADS):
        sc = sc + r[h * T:(h + 1) * T]
    o_ref[0] = sc


def _dsa_s_select_kernel(sp_ref, iq_ref, ikn_ref, misc_ref, bp_ref, bn_ref, *, G, T, topk):
    past = sp_ref.shape[2]
    R = G * T
    misc = misc_ref[...]
    iq = iq_ref[...]
    ikn = _pad_rows(ikn_ref[...], LANES).astype(BF16)
    sn = jnp.zeros((R, LANES), F32)
    for h in range(IDX_HEADS):
        s = _dot_nt(iq[:, h * LANES:(h + 1) * LANES].astype(BF16), ikn)
        sn = sn + misc[:, h:h + 1] * jnp.maximum(s, 0.0)
    row = lax.broadcasted_iota(I32, (R, LANES), 0)
    lane = lax.broadcasted_iota(I32, (R, LANES), 1)
    visible = (lane // T == row // T) & (lane % T <= row % T)
    key_new = jnp.where(visible, _order_key(sn), INT_MIN)
    key = jnp.concatenate([_order_key(sp_ref[...].reshape(R, past)), key_new], axis=1)
    idx = lax.broadcasted_iota(I32, key.shape, 1)
    sel = _topk_mask(key, idx, topk, 1, (past + LANES - 1).bit_length())
    bias = jnp.where(sel, 0.0, NEG_BIG)
    bp_ref[...] = bias[:, :past].reshape(G, T, past)
    bn_ref[...] = jnp.where(visible, bias[:, past:], NEG_BIG).reshape(G, T, LANES)


def _online_softmax_step(s, v, m_ref, l_ref, acc_ref, v_transposed=False):
    m_old = m_ref[...]
    m_new = jnp.maximum(m_old, jnp.max(s, axis=-1, keepdims=True))
    corr = jnp.exp(m_old - m_new)
    p = jnp.exp(s - m_new)
    l_ref[...] = l_ref[...] * corr + jnp.sum(p, axis=-1, keepdims=True)
    pv = _dot_nt(p.astype(BF16), v) if v_transposed else _dot(p.astype(BF16), v)
    acc_ref[...] = acc_ref[...] * corr + pv
    m_ref[...] = m_new


def _softmax_blocks_step(blocks, m_ref, l_ref, acc_ref, v_transposed=False):
    ms = [jnp.max(s, axis=-1, keepdims=True) for s, _ in blocks]
    ps = [jnp.exp(s - m_b) for (s, _), m_b in zip(blocks, ms)]
    ls = [jnp.sum(p, axis=-1, keepdims=True) for p in ps]
    pvs = [_dot_nt(p.astype(BF16), v) if v_transposed else _dot(p.astype(BF16), v)
           for p, (_, v) in zip(ps, blocks)]
    parts = list(zip(ms, ls, pvs))
    m_old = m_ref[...]
    m_new = m_old
    for m_b, _, _ in parts:
        m_new = jnp.maximum(m_new, m_b)
    corr = jnp.exp(m_old - m_new)
    l = l_ref[...] * corr
    acc = acc_ref[...] * corr
    for m_b, l_b, pv in parts:
        w = jnp.exp(m_b - m_new)
        l = l + l_b * w
        acc = acc + pv * w
    l_ref[...] = l
    acc_ref[...] = acc
    m_ref[...] = m_new


def _dsa_s_attend_kernel(pt_ref, aq_ref, akn_ref, avn_ref, bp_ref, bn_ref, *refs, n_pg, T):
    kpages, vpages = refs[:n_pg], refs[n_pg:2 * n_pg]
    o_ref, q_ref, m_ref, l_ref, acc_ref = refs[2 * n_pg:]
    j = pl.program_id(1)

    @pl.when(j == 0)
    def _():
        q_ref[...] = _stack_heads(aq_ref[...], A_HEADS).astype(BF16)
        m_ref[...] = jnp.full(m_ref.shape, NEG_BIG, F32)
        l_ref[...] = jnp.zeros(l_ref.shape, F32)
        acc_ref[...] = jnp.zeros(acc_ref.shape, F32)
        kn = _pad_rows(akn_ref[...], LANES).astype(BF16)
        vn = _pad_rows(avn_ref[...], LANES).astype(BF16)
        s = _dot_nt(q_ref[...], kn) + jnp.concatenate([bn_ref[0]] * A_HEADS, axis=0)
        _online_softmax_step(s, vn, m_ref, l_ref, acc_ref)

    q = q_ref[...]
    blocks = []
    for p0 in range(0, n_pg, PAGES_PER_BLOCK):
        group = range(p0, min(p0 + PAGES_PER_BLOCK, n_pg))
        k_t = jnp.concatenate([kpages[p][0, 0].astype(BF16) for p in group], axis=1)
        v_t = jnp.concatenate([vpages[p][0, 0].astype(BF16) for p in group], axis=1)
        ps = kpages[0].shape[3]
        bias = bp_ref[0, :, p0 * ps:(group[-1] + 1) * ps]
        blocks.append((_dot(q, k_t) + jnp.concatenate([bias] * A_HEADS, axis=0), v_t))
    _softmax_blocks_step(blocks, m_ref, l_ref, acc_ref, v_transposed=True)

    @pl.when(j == pl.num_programs(1) - 1)
    def _():
        o = acc_ref[...] / l_ref[...]
        lane = lax.broadcasted_iota(I32, (T, LANES), 1)
        rep = A_HEADS // A_KV_HEADS
        for hp in range(A_HEADS // 2):
            a = o[(2 * hp) * T:(2 * hp + 1) * T]
            b = o[(2 * hp + 1) * T:(2 * hp + 2) * T]
            if (2 * hp) // rep == 0:
                b = pltpu.roll(b, A_HEAD_DIM, 1)
            else:
                a = pltpu.roll(a, A_HEAD_DIM, 1)
            o_ref[:, hp * LANES:(hp + 1) * LANES] = jnp.where(lane < A_HEAD_DIM, a, b).astype(o_ref.dtype)


def _dsa_sample(iq, ikn, misc, aq, akn, avn, pool_k, pool_v, pool_idx, page_table, layer, *, Bd, T):
    n_pages = page_table.shape[1]
    page = pool_idx.shape[2]
    past = n_pages * page
    n_pg = math.gcd(PAGES_PER_STEP, n_pages)
    npg = n_pages // n_pg
    topk = min(TOPK_MAX, (past + T) // 4)
    W = A_HEADS * LANES
    kv_dim = A_KV_HEADS * A_HEAD_DIM
    pk = jnp.transpose(pool_k, (0, 1, 3, 4, 2)).reshape(pool_k.shape[:2] + (kv_dim, page))
    pv = jnp.transpose(pool_v, (0, 1, 3, 4, 2)).reshape(pool_v.shape[:2] + (kv_dim, page))
    pi = jnp.swapaxes(pool_idx, 2, 3)
    row = lambda w: pl.BlockSpec((T, w), lambda b, j, pt: (b, 0))

    n_sc = math.gcd(SCORE_PAGES_PER_STEP, n_pages)
    scores = pl.pallas_call(
        functools.partial(_dsa_s_score_kernel, n_pg=n_sc, T=T),
        grid_spec=pltpu.PrefetchScalarGridSpec(
            num_scalar_prefetch=1, grid=(Bd, n_pages // n_sc),
            in_specs=[row(W), row(LANES)] + _page_specs((1, 1, IDX_DIM, page), layer, n_sc),
            out_specs=pl.BlockSpec((1, T, n_sc * page), lambda b, j, pt: (b, 0, j))),
        out_shape=jax.ShapeDtypeStruct((Bd, T, past), F32),
        compiler_params=_params("parallel", "arbitrary"), name="dsa_s_score",
    )(page_table, iq, misc, *([pi] * n_sc))

    G = math.gcd(Bd, max(1, SELECT_ROWS // T))
    assert G * T <= LANES
    bias_p, bias_n = pl.pallas_call(
        functools.partial(_dsa_s_select_kernel, G=G, T=T, topk=topk),
        grid=(Bd // G,),
        in_specs=[pl.BlockSpec((G, T, past), lambda b: (b, 0, 0)),
                  pl.BlockSpec((G * T, W), lambda b: (b, 0)),
                  pl.BlockSpec((G * T, LANES), lambda b: (b, 0)),
                  pl.BlockSpec((G * T, LANES), lambda b: (b, 0))],
        out_specs=[pl.BlockSpec((G, T, past), lambda b: (b, 0, 0)),
                   pl.BlockSpec((G, T, LANES), lambda b: (b, 0, 0))],
        out_shape=[jax.ShapeDtypeStruct((Bd, T, past), F32),
                   jax.ShapeDtypeStruct((Bd, T, LANES), F32)],
        compiler_params=_params("parallel"), name="dsa_s_select",
    )(scores, iq, ikn, misc)

    group_rows = pl.BlockSpec((G * T, LANES), lambda b, j, pt: (b // G, 0))
    return pl.pallas_call(
        functools.partial(_dsa_s_attend_kernel, n_pg=n_pg, T=T),
        grid_spec=pltpu.PrefetchScalarGridSpec(
            num_scalar_prefetch=1, grid=(Bd, npg),
            in_specs=[row(W), group_rows, group_rows,
                      pl.BlockSpec((1, T, n_pg * page), lambda b, j, pt: (b, 0, j)),
                      pl.BlockSpec((1, T, LANES), lambda b, j, pt: (b, 0, 0))]
            + _page_specs((1, 1, kv_dim, page), layer, n_pg) * 2,
            out_specs=pl.BlockSpec((T, A_HEADS * A_HEAD_DIM), lambda b, j, pt: (b, 0)),
            scratch_shapes=[pltpu.VMEM((A_HEADS * T, LANES), BF16),
                            pltpu.VMEM((A_HEADS * T, 1), F32),
                            pltpu.VMEM((A_HEADS * T, 1), F32),
                            pltpu.VMEM((A_HEADS * T, LANES), F32)]),
        out_shape=jax.ShapeDtypeStruct((Bd * T, A_HEADS * A_HEAD_DIM), BF16),
        compiler_params=_params("parallel", "arbitrary"), name="dsa_s_attend",
    )(page_table, aq, akn, avn, bias_p, bias_n, *([pk] * n_pg), *([pv] * n_pg))


def _head_matmul_kernel(x_ref, w_ref, o_ref):
    o_ref[0] = _dot(x_ref[...].astype(BF16), w_ref[0])


def _mla_s_attend_kernel(pt_ref, ql_ref, q_ref, latn_ref, krn_ref, *refs, n_pg, T):
    lpages, rpages = refs[:n_pg], refs[n_pg:2 * n_pg]
    o_ref, qlat_ref, qr_ref, m_ref, l_ref, acc_ref = refs[2 * n_pg:]
    j = pl.program_id(1)
    R = C_HEADS * T
    lo = C_NOPE

    @pl.when(j == 0)
    def _():
        qlat_ref[...] = ql_ref[...].reshape(R, C_KV_LORA).astype(BF16)
        q = q_ref[...]
        qr_ref[...] = jnp.concatenate(
            [q[:, h * LANES + lo:h * LANES + lo + C_ROPE] for h in range(C_HEADS)], axis=0).astype(BF16)
        m_ref[...] = jnp.full(m_ref.shape, NEG_BIG, F32)
        l_ref[...] = jnp.zeros(l_ref.shape, F32)
        acc_ref[...] = jnp.zeros(acc_ref.shape, F32)
        latn = _pad_rows(latn_ref[...], LANES).astype(BF16)
        krn = _pad_rows(krn_ref[...][:, lo:lo + C_ROPE], LANES).astype(BF16)
        s = (_dot_nt(qlat_ref[...], latn) + _dot_nt(qr_ref[...], krn)) * MLA_SCALE
        t_row = lax.broadcasted_iota(I32, (R, LANES), 0) % T
        n_key = lax.broadcasted_iota(I32, (R, LANES), 1)
        s = jnp.where(n_key <= t_row, s, NEG_BIG)
        _online_softmax_step(s, latn, m_ref, l_ref, acc_ref)

    ql = qlat_ref[...]
    qr = qr_ref[...]
    blocks = []
    for p0 in range(0, n_pg, PAGES_PER_BLOCK):
        group = range(p0, min(p0 + PAGES_PER_BLOCK, n_pg))
        lat = jnp.concatenate([lpages[p][0, 0].astype(BF16) for p in group], axis=0)
        kr_t = jnp.concatenate([rpages[p][0, 0].astype(BF16) for p in group], axis=1)
        blocks.append(((_dot_nt(ql, lat) + _dot(qr, kr_t)) * MLA_SCALE, lat))
    _softmax_blocks_step(blocks, m_ref, l_ref, acc_ref)

    @pl.when(j == pl.num_programs(1) - 1)
    def _():
        o_ref[...] = (acc_ref[...] / l_ref[...]).reshape(C_HEADS, T, C_KV_LORA)


def _pair_matmul_kernel(a_ref, b_ref, wa_ref, wb_ref, o_ref):
    o_ref[...] = (_dot(a_ref[0].astype(BF16), wa_ref[0])
                  + _dot(b_ref[0].astype(BF16), wb_ref[0])).astype(o_ref.dtype)


def _mla_sample(q, latn, krn, w_ukv, pool_lat, pool_kr, page_table, layer, *, Bd, T):
    Ms = Bd * T
    n_pages = page_table.shape[1]
    page = pool_lat.shape[2]
    n_pg = math.gcd(PAGES_PER_STEP, n_pages)
    npg = n_pages // n_pg
    w_uk_t = jnp.pad(jnp.transpose(w_ukv[..., :C_NOPE], (1, 2, 0)),
                     ((0, 0), (0, LANES - C_NOPE), (0, 0))).astype(BF16)
    wv = jnp.transpose(w_ukv[..., C_NOPE:], (1, 0, 2))
    even = (jnp.arange(C_HEADS) % 2 == 0)[:, None, None]
    z = jnp.zeros_like(wv)
    w_uv = jnp.concatenate([jnp.where(even, wv, z), jnp.where(even, z, wv)], axis=-1).astype(BF16)

    q_lat = pl.pallas_call(
        _head_matmul_kernel, grid=(C_HEADS,),
        in_specs=[pl.BlockSpec((Ms, LANES), lambda h: (0, h)),
                  pl.BlockSpec((1, LANES, C_KV_LORA), lambda h: (h, 0, 0))],
        out_specs=pl.BlockSpec((1, Ms, C_KV_LORA), lambda h: (h, 0, 0)),
        out_shape=jax.ShapeDtypeStruct((C_HEADS, Ms, C_KV_LORA), F32),
        compiler_params=_params("parallel"), name="mla_s_qlat",
    )(q, w_uk_t)

    R = C_HEADS * T
    row = lambda w: pl.BlockSpec((T, w), lambda b, j, pt: (b, 0))
    o_lat = pl.pallas_call(
        functools.partial(_mla_s_attend_kernel, n_pg=n_pg, T=T),
        grid_spec=pltpu.PrefetchScalarGridSpec(
            num_scalar_prefetch=1, grid=(Bd, npg),
            in_specs=[pl.BlockSpec((C_HEADS, T, C_KV_LORA), lambda b, j, pt: (0, b, 0)),
                      row(C_HEADS * LANES), row(C_KV_LORA), row(LANES)]
            + _page_specs((1, 1, page, C_KV_LORA), layer, n_pg)
            + _page_specs((1, 1, C_ROPE, page), layer, n_pg),
            out_specs=pl.BlockSpec((C_HEADS, T, C_KV_LORA), lambda b, j, pt: (0, b, 0)),
            scratch_shapes=[pltpu.VMEM((R, C_KV_LORA), BF16), pltpu.VMEM((R, C_ROPE), BF16),
                            pltpu.VMEM((R, 1), F32), pltpu.VMEM((R, 1), F32),
                            pltpu.VMEM((R, C_KV_LORA), F32)]),
        out_shape=jax.ShapeDtypeStruct((C_HEADS, Ms, C_KV_LORA), F32),
        compiler_params=_params("parallel", "arbitrary"), name="mla_s_attend",
    )(page_table, q_lat, q, latn, krn, *([pool_lat] * n_pg), *([jnp.swapaxes(pool_kr, 2, 3)] * n_pg))

    return pl.pallas_call(
        _pair_matmul_kernel, grid=(C_HEADS // 2,),
        in_specs=[pl.BlockSpec((1, Ms, C_KV_LORA), lambda h: (2 * h, 0, 0)),
                  pl.BlockSpec((1, Ms, C_KV_LORA), lambda h: (2 * h + 1, 0, 0)),
                  pl.BlockSpec((1, C_KV_LORA, LANES), lambda h: (2 * h, 0, 0)),
                  pl.BlockSpec((1, C_KV_LORA, LANES), lambda h: (2 * h + 1, 0, 0))],
        out_specs=pl.BlockSpec((Ms, LANES), lambda h: (0, h)),
        out_shape=jax.ShapeDtypeStruct((Ms, C_HEADS * C_V), BF16),
        compiler_params=_params("parallel"), name="mla_s_out",
    )(o_lat, o_lat, w_uv, w_uv)


def _mla_weights(w_down, w_uq, w_ukv):
    D = w_down.shape[0]
    kr_cols = w_down[:, C_Q_LORA + C_KV_LORA:]
    wd = jnp.concatenate([w_down[:, :C_Q_LORA + C_KV_LORA], jnp.zeros((D, C_NOPE), F32), kr_cols,
                          jnp.zeros((D, LANES - C_NOPE - C_ROPE), F32)], axis=1)
    wq = jnp.pad(w_uq, ((0, 0), (0, 0), (0, LANES - C_NOPE - C_ROPE))).reshape(C_Q_LORA, C_HEADS * LANES)
    wkv = w_ukv.reshape(C_KV_LORA, C_HEADS * LANES)
    return wd.astype(BF16), wq.astype(BF16), wkv.astype(BF16)


def _mla_project(x, g_mix, wd, wq, wkv, g_q, g_kv, rope, tm, qdtype):
    cq, ckv, kr = _proj(x, wd, ((C_Q_LORA, False, F32), (C_KV_LORA, False, F32), (LANES, True, F32)),
                        gain=g_mix, rope=rope, tm=tm, name="mla_down")
    (q,) = _proj(cq, wq, ((C_HEADS * LANES, True, qdtype),), gain=g_q, rope=rope, tm=tm, name="mla_uq")
    kv, lat = _proj(ckv, wkv, ((C_HEADS * LANES, False, BF16),), gain=g_kv, emit_norm=True, tm=tm,
                    name="mla_ukv")
    return q, kv, lat, kr


def _even_weights(w_in):
    D = w_in.shape[0]
    sizes = (A_HEADS * A_HEAD_DIM, A_KV_HEADS * A_HEAD_DIM, A_KV_HEADS * A_HEAD_DIM,
             IDX_HEADS * IDX_DIM, IDX_DIM, IDX_HEADS, GDN_QK, GDN_QK, GDN_QK, GDN_QK, B_HEADS, B_HEADS)
    parts, c = [], 0
    for s in sizes:
        parts.append(w_in[:, c:c + s])
        c += s
    aq, ak, av, iq, ik, iw, bq, bk, bv, bz, bb, ba = parts
    zero = jnp.zeros((D, A_HEADS, A_HEAD_DIM), F32)
    aq = aq.reshape(D, A_HEADS, A_HEAD_DIM)
    in_g0 = (jnp.arange(A_HEADS) < A_HEADS // A_KV_HEADS)[None, :, None]
    aq128 = jnp.concatenate([jnp.where(in_g0, aq, zero), jnp.where(in_g0, zero, aq)], axis=-1)
    iq128 = jnp.concatenate([iq.reshape(D, IDX_HEADS, IDX_DIM), zero], axis=-1)
    ik128 = jnp.pad(ik, ((0, 0), (0, LANES - IDX_DIM)))
    misc = jnp.pad(jnp.concatenate([iw, bb, ba], axis=1), ((0, 0), (0, LANES - IDX_HEADS - 2 * B_HEADS)))
    w = jnp.concatenate([aq128.reshape(D, -1), ak, iq128.reshape(D, -1), ik128, av, misc, bq, bk, bv, bz],
                        axis=1)
    misc_col = A_HEADS * LANES + LANES + IDX_HEADS * LANES + LANES + LANES
    scale = jnp.ones((w.shape[1],), F32).at[misc_col:misc_col + IDX_HEADS].set(IDX_W_SCALE)
    assert math.frexp(A_SCALE)[0] == 0.5
    scale = scale.at[:A_HEADS * LANES].set(A_SCALE)
    return w.astype(BF16), scale


def _even_groups(qdtype, kdtype, t_rows):
    return ((A_HEADS * LANES, True, qdtype), (LANES, True, kdtype, t_rows),
            (IDX_HEADS * LANES, True, qdtype), (LANES, True, kdtype, t_rows),
            (LANES, False, None if t_rows else F32, t_rows), (LANES, False, F32),
            (GDN_CONV_DIM, False, F32), (GDN_QK, False, F32))


PROMPT_ROW_TILE = 512


def _tile_rows(t, rows):
    return jnp.tile(t, (rows // t.shape[0], 1))


def kernel(x_prompt, x_sample, cache_a_k, cache_a_v, cache_a_idx, state_b_ssm, state_b_conv,
           cache_c_latent, cache_c_krope, cache_mem_k, cache_mem_v, page_table, mem_prompt,
           g_mix, g_cross, g_mem, g_mlp, g_final,
           w_in_even, gdn_conv_w, gdn_a_log, gdn_dt_bias, gdn_norm_g, w_out_even,
           w_down_odd, g_q_lora, g_kv_lora, w_uq, w_ukv, w_out_odd,
           w_xq, w_xk, w_xv, w_xo, w_ff1, w_ff2):
    Bp, S, D = x_prompt.shape
    Bd, T, _ = x_sample.shape
    depth = g_mix.shape[0]
    Mp, Ms = Bp * S, Bd * T
    past_len = page_table.shape[1] * cache_a_k.shape[2]
    pos_p = jnp.arange(S, dtype=jnp.int32)
    pos_s = past_len + jnp.arange(T, dtype=jnp.int32)
    tm_p = min(PROMPT_ROW_TILE, S)
    tm_s = min(256, Ms)

    def rope_pair(dim, lo, hi):
        cp, sap, sbp, half = _rope_tables(pos_p, dim, lo, hi)
        cs, sas, sbs, _ = _rope_tables(pos_s, dim, lo, hi)
        return ((cp, sap, sbp, half, S // tm_p),
                (_tile_rows(cs, tm_s), _tile_rows(sas, tm_s), _tile_rows(sbs, tm_s), half, 1))

    rope_even_p, rope_even_s = rope_pair(A_HEAD_DIM, 0, LANES)
    rope_odd_p, rope_odd_s = rope_pair(C_ROPE, C_NOPE, C_NOPE + C_ROPE)

    xp = x_prompt.reshape(Mp, D)
    xs = x_sample.reshape(Ms, D)
    mem = mem_prompt.reshape(Bp * N_MEM, D)
    res = ((D, False, F32),)
    pa_k, pa_v, pa_i, pb_s, pb_c, pc_l, pc_r, pm_k, pm_v = [], [], [], [], [], [], [], [], []
    sa_k, sa_v, sa_i, sb_s, sb_c, sc_l, sc_r = [], [], [], [], [], [], []
    for li in range(depth):
        if li % 2 == 0:
            e = li // 2
            w_even, cscale = _even_weights(w_in_even[e])
            w_out = w_out_even[e].astype(BF16)
            gdn_w = (gdn_conv_w[e], gdn_a_log[e], gdn_dt_bias[e], gdn_norm_g[e])
            aq, ak, iq, ik, misc, conv_in, bz, ak_t, ik_t, av_t = _proj(
                xp, w_even, _even_groups(BF16, BF16, S), gain=g_mix[li], rope=rope_even_p,
                colscale=cscale, tm=tm_p, name="even_in")
            a_out = _dsa_prompt(iq, ik, misc[:, :IDX_HEADS].T, aq, ak, av_t, B=Bp, S=S)
            b_out, tail, sbd = _gdn(
                conv_in, bz, misc, jnp.zeros((Bp, GDN_TAIL, GDN_CONV_DIM), F32),
                jnp.zeros((Bp, B_HEADS // 2, LANES, LANES), F32), *gdn_w, B=Bp, L=S)
            (xp,) = _proj(jnp.concatenate([a_out, b_out], axis=1), w_out, res, residual=xp, tm=tm_p,
                          name="even_out")
            from_t = lambda a: jnp.transpose(a.reshape(Bp, A_KV_HEADS, A_HEAD_DIM, S), (0, 3, 1, 2))
            pa_k.append(from_t(ak_t))
            pa_v.append(from_t(av_t))
            pa_i.append(jnp.swapaxes(ik_t[:, :IDX_DIM, :], 1, 2))
            pb_s.append(_state_from_blockdiag(sbd))
            pb_c.append(tail[:, GDN_TAIL - (CONV_WIDTH - 1):])
            aq, ak, iq, ik, av, misc, conv_in, bz = _proj(
                xs, w_even, _even_groups(F32, F32, 0), gain=g_mix[li], rope=rope_even_s, colscale=cscale,
                tm=tm_s, name="even_in_s")
            a_out = _dsa_sample(iq, ik, misc, aq, ak, av, cache_a_k, cache_a_v, cache_a_idx,
                                page_table, e, Bd=Bd, T=T)
            tail0 = jnp.pad(state_b_conv[e], ((0, 0), (GDN_TAIL - (CONV_WIDTH - 1), 0), (0, 0)))
            b_out, tail, sbd = _gdn(conv_in, bz, misc, tail0, _state_to_blockdiag(state_b_ssm[e]),
                                    *gdn_w, B=Bd, L=T)
            (xs,) = _proj(jnp.concatenate([a_out, b_out], axis=1), w_out, res, residual=xs, tm=tm_s,
                          name="even_out_s")
            sa_k.append(ak.reshape(Bd, T, A_KV_HEADS, A_HEAD_DIM))
            sa_v.append(av.reshape(Bd, T, A_KV_HEADS, A_HEAD_DIM))
            sa_i.append(ik[:, :IDX_DIM].reshape(Bd, T, IDX_DIM))
            sb_s.append(_state_from_blockdiag(sbd).astype(state_b_ssm.dtype))
            sb_c.append(tail[:, GDN_TAIL - (CONV_WIDTH - 1):])
        else:
            o = li // 2
            wd, wq, wkv = _mla_weights(w_down_odd[o], w_uq[o], w_ukv[o])
            w_out = w_out_odd[o].astype(BF16)
            q, kv, lat, kr = _mla_project(xp, g_mix[li], wd, wq, wkv, g_q_lora[o], g_kv_lora[o],
                                          rope_odd_p, tm_p, BF16)
            (xp,) = _proj(_mla_prompt(q, kv, kr, B=Bp, S=S), w_out, res, residual=xp, tm=tm_p,
                          name="odd_out")
            pc_l.append(lat.reshape(Bp, S, C_KV_LORA))
            pc_r.append(kr[:, C_NOPE:C_NOPE + C_ROPE].reshape(Bp, S, C_ROPE))
            q, kv, lat, kr = _mla_project(xs, g_mix[li], wd, wq, wkv, g_q_lora[o], g_kv_lora[o],
                                          rope_odd_s, tm_s, F32)
            a_out = _mla_sample(q, lat, kr, w_ukv[o], cache_c_latent, cache_c_krope, page_table, o,
                                Bd=Bd, T=T)
            (xs,) = _proj(a_out, w_out, res, residual=xs, tm=tm_s, name="odd_out_s")
            sc_l.append(lat.reshape(Bd, T, C_KV_LORA))
            sc_r.append(kr[:, C_NOPE:C_NOPE + C_ROPE].reshape(Bd, T, C_ROPE))
        w_kv = jnp.concatenate([w_xk[li], w_xv[li]], axis=1).astype(BF16)
        mk, mv = _proj(mem, w_kv, ((X_WIDTH, False, F32), (X_WIDTH, False, F32)), gain=g_mem[li],
                       tm=min(256, mem.shape[0]), name="mem_kv")
        pm_k.append(mk.reshape(Bp, N_MEM, X_HEADS, X_HEAD_DIM))
        pm_v.append(mv.reshape(Bp, N_MEM, X_HEADS, X_HEAD_DIM))
        wxq, wxo = w_xq[li].astype(BF16), w_xo[li].astype(BF16)
        xp = _xattn(xp, g_cross[li], wxq, mk.reshape(Bp, N_MEM, X_WIDTH), mv.reshape(Bp, N_MEM, X_WIDTH),
                    wxo, rows_per_batch=S, tm=tm_p)
        xs = _xattn_rows(xs, g_cross[li], wxq, cache_mem_k, cache_mem_v, wxo, li, T=T,
                         G=math.gcd(Bd, XATTN_ROWS_PER_STEP))
        w1, w2 = w_ff1[li].astype(BF16), w_ff2[li].astype(BF16)
        last = li == depth - 1
        xp = _mlp(xp, g_mlp[li], w1, w2, g_final, final_norm=last)
        xs = _mlp(xs, g_mlp[li], w1, w2, g_final, final_norm=last)
    return (xp.reshape(Bp, S, D), xs.reshape(Bd, T, D),
            jnp.stack(pa_k, axis=1), jnp.stack(pa_v, axis=1), jnp.stack(pa_i, axis=1),
            jnp.stack(pb_s, axis=0), jnp.stack(pb_c, axis=0),
            jnp.stack(pc_l, axis=1), jnp.stack(pc_r, axis=1),
            jnp.stack(pm_k, axis=0), jnp.stack(pm_v, axis=0),
            jnp.stack(sa_k, axis=1), jnp.stack(sa_v, axis=1), jnp.stack(sa_i, axis=1),
            jnp.stack(sb_s, axis=0), jnp.stack(sb_c, axis=0),
            jnp.stack(sc_l, axis=1), jnp.stack(sc_r, axis=1))
```
